```python
import functools
import jax, jax.numpy as jnp
from jax import lax
import numpy as np

D_MODEL = 1024
BATCH = 16
SEQ = 2048
DEPTH = 1
DEC_BATCH = 32
DEC_SEQ = 8
PAST_LEN = 16384
PAGE_SIZE = 128

HEAD_DIM = 64
MIX_WIDTH = D_MODEL
ATTN_HEADS = MIX_WIDTH // (2 * HEAD_DIM)
GDN_HEADS = MIX_WIDTH // (2 * HEAD_DIM)
ATTN_WIDTH = ATTN_HEADS * HEAD_DIM
GDN_WIDTH = GDN_HEADS * HEAD_DIM
GDN_QKV = 3 * GDN_WIDTH
OFF_QA = 0
OFF_KA = OFF_QA + ATTN_WIDTH
OFF_VA = OFF_KA + ATTN_WIDTH
OFF_G = OFF_VA + ATTN_WIDTH
OFF_Z = OFF_G + GDN_QKV
OFF_A = OFF_Z + GDN_WIDTH
OFF_B = OFF_A + GDN_HEADS
IN_COLS = OFF_B + GDN_HEADS
MOBA_BLOCK = 256
MOBA_TOPK = 3
Q_CHUNK = 128
CONV_WIDTH = 4
GDN_CHUNK = 64
N_EXPERTS = 32
TOP_K = 4
D_FF = D_MODEL
SWIGLU_LIMIT = 7.0
SWIGLU_ALPHA = 1.702
EXPERT_BLOCK = 256
EPS = 1e-6

kernel_name = 'hymba_moba_gdn_moe_step'


def rms_norm(x, g):
    xf = x.astype(jnp.float32)
    y = xf * lax.rsqrt(jnp.mean(xf * xf, axis=-1, keepdims=True) + EPS)
    return (y * g.astype(jnp.float32)).astype(x.dtype)


def l2_norm(x):
    return x * lax.rsqrt(jnp.sum(x * x, axis=-1, keepdims=True) + EPS)


def alibi_slopes(n_heads):
    return 2.0 ** (-8.0 * jnp.arange(1, n_heads + 1, dtype=jnp.float32) / n_heads)


def moba_prompt(q, k, v):
    b, s, h, d = q.shape
    nb = -(-s // MOBA_BLOCK)
    n_qc = s // Q_CHUNK
    n_sel = min(MOBA_TOPK, nb - 1)
    pad = ((0, 0), (0, nb * MOBA_BLOCK - s), (0, 0), (0, 0))
    kp = jnp.pad(k.astype(jnp.float32), pad).reshape(b, nb, MOBA_BLOCK, h, d)
    vp = jnp.pad(v.astype(jnp.float32), pad).reshape(b, nb, MOBA_BLOCK, h, d)
    kmean = jnp.mean(kp, axis=2)
    slopes = alibi_slopes(h)
    scale = d ** -0.5
    hidx = jnp.arange(h)[:, None, None, None]
    blk = jnp.arange(MOBA_BLOCK)

    def one_chunk(idx):
        bi = idx // n_qc
        start = (idx % n_qc) * Q_CHUNK
        qc = lax.dynamic_slice_in_dim(q[bi], start, Q_CHUNK, axis=0).astype(jnp.float32)
        pos_q = start + jnp.arange(Q_CHUNK)
        ob = start // MOBA_BLOCK
        kb, vb = kp[bi], vp[bi]
        dist_own = (pos_q[:, None] - (ob * MOBA_BLOCK + blk)[None, :]).astype(jnp.float32)
        lg_own = jnp.einsum('qhd,khd->hqk', qc, kb[ob]) * scale - slopes[:, None, None] * dist_own
        lg_own = jnp.where(dist_own >= 0, lg_own, -jnp.inf)
        if n_sel == 0:
            p = jax.nn.softmax(lg_own, axis=-1)
            return jnp.einsum('hqk,khd->qhd', p, vb[ob]).astype(q.dtype)
        gate = jnp.einsum('qhd,nhd->hqn', qc, kmean[bi])
        gate = jnp.where(jnp.arange(nb) < ob, gate, -jnp.inf)
        _, sel = lax.top_k(gate, n_sel)
        k_sel = kb[sel[..., None], blk, hidx]
        v_sel = vb[sel[..., None], blk, hidx]
        dist_sel = (pos_q[None, :, None, None] - (sel[..., None] * MOBA_BLOCK + blk)).astype(jnp.float32)
        lg_sel = jnp.einsum('qhd,hqnkd->hqnk', qc, k_sel) * scale - slopes[:, None, None, None] * dist_sel
        lg_sel = jnp.where((sel < ob)[..., None], lg_sel, -jnp.inf)
        p = jax.nn.softmax(jnp.concatenate([lg_own, lg_sel.reshape(h, Q_CHUNK, n_sel * MOBA_BLOCK)], axis=-1), axis=-1)
        p_sel = p[..., MOBA_BLOCK:].reshape(h, Q_CHUNK, n_sel, MOBA_BLOCK)
        o = jnp.einsum('hqk,khd->qhd', p[..., :MOBA_BLOCK], vb[ob]) + jnp.einsum('hqnk,hqnkd->qhd', p_sel, v_sel)
        return o.astype(q.dtype)

    out = lax.map(one_chunk, jnp.arange(b * n_qc))
    return out.reshape(b, s, h, d)


def moba_sample(q, k, v, cache_k, cache_v, page_table):
    f32 = jnp.float32
    b, l, h, d = q.shape
    n_pages = page_table.shape[1]
    past = n_pages * PAGE_SIZE
    n_full = past // MOBA_BLOCK
    own_start = n_full * MOBA_BLOCK
    n_sel = min(MOBA_TOPK, n_full)
    slopes = alibi_slopes(h)
    scale = d ** -0.5
    qf = q.astype(f32)
    pos_q = past + jnp.arange(l)
    k_past = cache_k[page_table].reshape(b, past, h, d).astype(f32)
    k_own = jnp.concatenate([k_past[:, own_start:], k.astype(f32)], axis=1)
    v_own_past = cache_v[page_table[:, own_start // PAGE_SIZE:]].reshape(b, past - own_start, h, d)
    v_own = jnp.concatenate([v_own_past.astype(f32), v.astype(f32)], axis=1)
    r = past - own_start + l
    dist_own = (pos_q[:, None] - (own_start + jnp.arange(r))[None, :]).astype(f32)
    lg_own = jnp.einsum('blhd,brhd->bhlr', qf, k_own) * scale - slopes[:, None, None] * dist_own
    lg_own = jnp.where(dist_own >= 0, lg_own, -jnp.inf)
    if n_sel == 0:
        p = jax.nn.softmax(lg_own, axis=-1)
        return jnp.einsum('bhlr,brhd->blhd', p, v_own).astype(q.dtype)
    kmean = jnp.mean(k_past[:, :own_start].reshape(b, n_full, MOBA_BLOCK, h, d), axis=2)
    gate = jnp.einsum('blhd,bnhd->bhln', qf, kmean)
    _, sel = lax.top_k(gate, n_sel)
    ppb = MOBA_BLOCK // PAGE_SIZE
    logical = sel[..., None] * ppb + jnp.arange(ppb)
    phys = page_table[jnp.arange(b)[:, None, None, None, None], logical]
    hidx = jnp.arange(h)[None, :, None, None, None, None]
    prow = jnp.arange(PAGE_SIZE)
    k_sel = cache_k[phys[..., None], prow, hidx].reshape(b, h, l, n_sel, MOBA_BLOCK, d).astype(f32)
    v_sel = cache_v[phys[..., None], prow, hidx].reshape(b, h, l, n_sel, MOBA_BLOCK, d).astype(f32)
    dist_sel = (pos_q[None, None, :, None, None] - (sel[..., None] * MOBA_BLOCK + jnp.arange(MOBA_BLOCK))).astype(f32)
    lg_sel = jnp.einsum('blhd,bhlnkd->bhlnk', qf, k_sel) * scale - slopes[None, :, None, None, None] * dist_sel
    p = jax.nn.softmax(jnp.concatenate([lg_own, lg_sel.reshape(b, h, l, n_sel * MOBA_BLOCK)], axis=-1), axis=-1)
    p_sel = p[..., r:].reshape(b, h, l, n_sel, MOBA_BLOCK)
    o = jnp.einsum('bhlr,brhd->blhd', p[..., :r], v_own) + jnp.einsum('bhlnk,bhlnkd->blhd', p_sel, v_sel)
    return o.astype(q.dtype)


def chunk_gated_delta(q, k, v, g, beta, s0):
    f32 = jnp.float32
    b, l, h, dk = q.shape
    dv = v.shape[-1]
    c = GDN_CHUNK
    n = -(-l // c)
    pad = n * c - l

    def prep(t):
        t = jnp.pad(t.astype(f32), ((0, 0), (0, pad)) + ((0, 0),) * (t.ndim - 2))
        t = jnp.moveaxis(t, 2, 1)
        return t.reshape((b, h, n, c) + t.shape[3:])

    q = prep(q) * dk ** -0.5
    k, v, g, beta = prep(k), prep(v), prep(g), prep(beta)
    gc = jnp.cumsum(g, axis=-1)
    causal = jnp.tril(jnp.ones((c, c), bool))
    strict = jnp.tril(jnp.ones((c, c), bool), -1)
    decay = jnp.exp(jnp.where(causal, gc[..., :, None] - gc[..., None, :], -jnp.inf))
    kb = k * beta[..., None]
    vb = v * beta[..., None]
    a_mat = jnp.where(strict, jnp.einsum('bhnid,bhnjd->bhnij', kb, k) * decay, 0.0)
    eye = jnp.eye(c, dtype=f32)
    t_mat = lax.linalg.triangular_solve(eye + a_mat, jnp.broadcast_to(eye, a_mat.shape),
                                        left_side=True, lower=True, unit_diagonal=True)
    u = t_mat @ vb
    w = t_mat @ (kb * jnp.exp(gc)[..., None])
    qk = jnp.where(causal, jnp.einsum('bhnid,bhnjd->bhnij', q, k) * decay, 0.0)

    def step(s, xs):
        qi, ki, ui, wi, gi, qki = xs
        v_new = ui - wi @ s
        o = (qi * jnp.exp(gi)[..., None]) @ s + qki @ v_new
        g_last = gi[..., -1]
        s = s * jnp.exp(g_last)[..., None, None] + jnp.einsum(
            'bhcd,bhce->bhde', ki * jnp.exp(g_last[..., None] - gi)[..., None], v_new)
        return s, o

    xs = tuple(jnp.moveaxis(t, 2, 0) for t in (q, k, u, w, gc, qk))
    s_fin, o = lax.scan(step, s0.astype(f32), xs)
    o = jnp.moveaxis(o, 0, 2).reshape(b, h, n * c, dv)[:, :, :l]
    return jnp.moveaxis(o, 1, 2), s_fin


def gdn_mixer(qkv, z, a, bt, conv_buf, s0, conv_w, a_log, dt_bias, norm_g):
    f32 = jnp.float32
    b, l, _ = qkv.shape
    xp = jnp.concatenate([conv_buf.astype(qkv.dtype), qkv], axis=1)
    conv = xp[:, 0:l] * conv_w[0]
    for j in range(1, CONV_WIDTH):
        conv = conv + xp[:, j:j + l] * conv_w[j]
    act = jax.nn.silu(conv).astype(f32).reshape(b, l, 3, GDN_HEADS, HEAD_DIM)
    q, k, v = l2_norm(act[:, :, 0]), l2_norm(act[:, :, 1]), act[:, :, 2]
    beta = jax.nn.sigmoid(bt.astype(f32))
    g = -jnp.exp(a_log.astype(f32)) * jax.nn.softplus(a.astype(f32) + dt_bias.astype(f32))
    o, s_new = chunk_gated_delta(q, k, v, g, beta, s0)
    o = rms_norm(o, norm_g) * jax.nn.silu(z.astype(f32).reshape(b, l, GDN_HEADS, HEAD_DIM))
    return o.reshape(b, l, GDN_WIDTH).astype(qkv.dtype), s_new, xp[:, -(CONV_WIDTH - 1):]


def moe(x2, w_router, b_router, w_gate_up, b_gate_up, w_down, b_down):
    t, d = x2.shape
    logits = (jnp.einsum('td,de->te', x2, w_router) + b_router).astype(jnp.float32)
    top_v, top_i = lax.top_k(logits, TOP_K)
    gates = jax.nn.softmax(top_v, axis=-1)
    flat_e = top_i.reshape(-1)
    order = jnp.argsort(flat_e)
    sorted_e = flat_e[order]
    tok = order // TOP_K
    counts = jnp.bincount(flat_e, length=N_EXPERTS)
    padded = (counts + EXPERT_BLOCK - 1) // EXPERT_BLOCK * EXPERT_BLOCK
    pad_end = jnp.cumsum(padded)
    pad_start = pad_end - padded
    start = jnp.cumsum(counts) - counts
    dest = pad_start[sorted_e] + jnp.arange(t * TOP_K) - start[sorted_e]
    n_blocks = (t * TOP_K + N_EXPERTS * (EXPERT_BLOCK - 1)) // EXPERT_BLOCK + 1
    block_e = jnp.minimum(jnp.searchsorted(pad_end, jnp.arange(n_blocks) * EXPERT_BLOCK, side='right'), N_EXPERTS - 1)
    xbuf = jnp.zeros((n_blocks * EXPERT_BLOCK, d), x2.dtype).at[dest].set(x2[tok])

    def expert(args):
        xb, e = args
        gu = xb @ w_gate_up[e] + b_gate_up[e]
        gl = jnp.minimum(gu[:, 0::2], SWIGLU_LIMIT)
        lin = jnp.clip(gu[:, 1::2], -SWIGLU_LIMIT, SWIGLU_LIMIT)
        hmid = gl * jax.nn.sigmoid(SWIGLU_ALPHA * gl) * (lin + 1.0)
        return hmid @ w_down[e] + b_down[e]

    ybuf = lax.map(expert, (xbuf.reshape(n_blocks, EXPERT_BLOCK, d), block_e)).reshape(-1, d)
    w_sorted = gates.reshape(-1)[order]
    y = jnp.zeros((t, d), jnp.float32).at[tok].add(w_sorted[:, None] * ybuf[dest].astype(jnp.float32))
    return y.astype(x2.dtype)


def layer_forward(h, attend, conv_buf, s0, norm1_g, w_in, q_norm_g, k_norm_g, conv_w, a_log, dt_bias,
                  gdn_norm_g, w_out, norm2_g, w_router, b_router, w_gate_up, b_gate_up, w_down, b_down):
    b, l, _ = h.shape
    n = rms_norm(h, norm1_g)
    p = jnp.einsum('bld,de->ble', n, w_in)
    qa = rms_norm(p[..., OFF_QA:OFF_KA].reshape(b, l, ATTN_HEADS, HEAD_DIM), q_norm_g)
    ka = rms_norm(p[..., OFF_KA:OFF_VA].reshape(b, l, ATTN_HEADS, HEAD_DIM), k_norm_g)
    va = p[..., OFF_VA:OFF_G].reshape(b, l, ATTN_HEADS, HEAD_DIM)
    att = attend(qa, ka, va).reshape(b, l, ATTN_WIDTH)
    gdn, s_new, conv_new = gdn_mixer(p[..., OFF_G:OFF_Z], p[..., OFF_Z:OFF_A], p[..., OFF_A:OFF_B],
                                     p[..., OFF_B:IN_COLS], conv_buf, s0, conv_w, a_log, dt_bias, gdn_norm_g)
    h = h + jnp.einsum('blm,md->bld', jnp.concatenate([att, gdn], axis=-1), w_out)
    m = rms_norm(h, norm2_g).reshape(b * l, D_MODEL)
    h = h + moe(m, w_router, b_router, w_gate_up, b_gate_up, w_down, b_down).reshape(b, l, D_MODEL)
    return h, ka, va, s_new, conv_new


def setup_inputs(seed: int = 0) -> dict:
    key = jax.random.key(seed)
    ks = jax.random.split(key, 24)
    f32 = jnp.float32
    n_pages = PAST_LEN // PAGE_SIZE
    n_used = DEC_BATCH * n_pages
    n_pool = n_used + n_used // 4

    def nrm(k, shape, s):
        return jax.random.normal(k, shape, f32) * s

    page_table = jax.random.permutation(ks[4], n_pool)[:n_used].reshape(DEC_BATCH, n_pages).astype(jnp.int32)
    return {
        'x_prompt': nrm(ks[0], (BATCH, SEQ, D_MODEL), 1.0),
        'x_sample': nrm(ks[1], (DEC_BATCH, DEC_SEQ, D_MODEL), 1.0),
        'cache_k': nrm(ks[2], (DEPTH, n_pool, PAGE_SIZE, ATTN_HEADS, HEAD_DIM), 1.0),
        'cache_v': nrm(ks[3], (DEPTH, n_pool, PAGE_SIZE, ATTN_HEADS, HEAD_DIM), 1.0),
        'page_table': page_table,
        'state_gdn': nrm(ks[5], (DEPTH, DEC_BATCH, GDN_HEADS, HEAD_DIM, HEAD_DIM), 0.1),
        'state_conv': nrm(ks[6], (DEPTH, DEC_BATCH, CONV_WIDTH - 1, GDN_QKV), 1.0),
        'norm1_g': 1.0 + nrm(ks[7], (DEPTH, D_MODEL), 0.05),
        'w_in': nrm(ks[8], (DEPTH, D_MODEL, IN_COLS), D_MODEL ** -0.5),
        'q_norm_g': 1.0 + nrm(ks[9], (DEPTH, HEAD_DIM), 0.05),
        'k_norm_g': 1.0 + nrm(ks[10], (DEPTH, HEAD_DIM), 0.05),
        'conv_w': nrm(ks[11], (DEPTH, CONV_WIDTH, GDN_QKV), CONV_WIDTH ** -0.5),
        'a_log': jnp.log(jax.random.uniform(ks[12], (DEPTH, GDN_HEADS), f32, 1.0, 16.0)),
        'dt_bias': nrm(ks[13], (DEPTH, GDN_HEADS), 0.1),
        'gdn_norm_g': 1.0 + nrm(ks[14], (DEPTH, HEAD_DIM), 0.05),
        'w_out': nrm(ks[15], (DEPTH, MIX_WIDTH, D_MODEL), MIX_WIDTH ** -0.5),
        'norm2_g': 1.0 + nrm(ks[16], (DEPTH, D_MODEL), 0.05),
        'w_router': nrm(ks[17], (DEPTH, D_MODEL, N_EXPERTS), D_MODEL ** -0.5),
        'b_router': nrm(ks[18], (DEPTH, N_EXPERTS), 0.01),
        'w_gate_up': nrm(ks[19], (DEPTH, N_EXPERTS, D_MODEL, 2 * D_FF), D_MODEL ** -0.5),
        'b_gate_up': nrm(ks[20], (DEPTH, N_EXPERTS, 2 * D_FF), 0.01),
        'w_down': nrm(ks[21], (DEPTH, N_EXPERTS, D_FF, D_MODEL), D_FF ** -0.5),
        'b_down': nrm(ks[22], (DEPTH, N_EXPERTS, D_MODEL), 0.01),
    }


def reference(x_prompt, x_sample, cache_k, cache_v, page_table, state_gdn, state_conv, norm1_g, w_in,
              q_norm_g, k_norm_g, conv_w, a_log, dt_bias, gdn_norm_g, w_out, norm2_g, w_router, b_router,
              w_gate_up, b_gate_up, w_down, b_down):
    b, s, _ = x_prompt.shape
    hp, hs = x_prompt, x_sample
    kp_l, vp_l, ks_l, vs_l, gp_l, cp_l, gs_l, cs_l = [], [], [], [], [], [], [], []
    for i in range(DEPTH):
        lw = (norm1_g[i], w_in[i], q_norm_g[i], k_norm_g[i], conv_w[i], a_log[i], dt_bias[i], gdn_norm_g[i],
              w_out[i], norm2_g[i], w_router[i], b_router[i], w_gate_up[i], b_gate_up[i], w_down[i], b_down[i])
        conv0 = jnp.zeros((b, CONV_WIDTH - 1, GDN_QKV), x_prompt.dtype)
        s00 = jnp.zeros((b, GDN_HEADS, HEAD_DIM, HEAD_DIM), jnp.float32)
        hp, ka, va, s_new, c_new = layer_forward(hp, moba_prompt, conv0, s00, *lw)
        kp_l.append(ka.reshape(b, s // PAGE_SIZE, PAGE_SIZE, ATTN_HEADS, HEAD_DIM))
        vp_l.append(va.reshape(b, s // PAGE_SIZE, PAGE_SIZE, ATTN_HEADS, HEAD_DIM))
        gp_l.append(s_new)
        cp_l.append(c_new)
        attend_s = functools.partial(moba_sample, cache_k=cache_k[i], cache_v=cache_v[i], page_table=page_table)
        hs, ka, va, s_new, c_new = layer_forward(hs, attend_s, state_conv[i], state_gdn[i], *lw)
        ks_l.append(ka)
        vs_l.append(va)
        gs_l.append(s_new)
        cs_l.append(c_new)
    k_prompt, v_prompt = jnp.stack(kp_l), jnp.stack(vp_l)
    k_sample, v_sample = jnp.stack(ks_l), jnp.stack(vs_l)
    gdn_prompt, conv_prompt = jnp.stack(gp_l), jnp.stack(cp_l)
    gdn_sample, conv_sample = jnp.stack(gs_l), jnp.stack(cs_l)
    return (hp, hs, k_prompt, v_prompt, k_sample, v_sample, gdn_prompt, conv_prompt, gdn_sample, conv_sample)
```

```python
import functools

import jax
import jax.numpy as jnp
import numpy as np
from jax import lax
from jax.experimental import pallas as pl
from jax.experimental.pallas import tpu as pltpu

F32 = jnp.float32
BF16 = jnp.bfloat16

LANES = 128
SUBLANES = 8
HEAD_DIM = 64
N_HEADS = 8
PAIR = 2 * HEAD_DIM
N_PAIRS = N_HEADS // 2
WIDTH = N_HEADS * HEAD_DIM
D_MODEL = 1024
GDN_QKV = 3 * WIDTH
IN_COLS = 3 * WIDTH + GDN_QKV + WIDTH + 2 * N_HEADS
IN_COLS_PAD = 3712
OFF_GQ = 3 * WIDTH
OFF_Z = OFF_GQ + GDN_QKV
OFF_AB = OFF_Z + WIDTH
MOBA_BLOCK = 256
MOBA_TOPK = 3
Q_CHUNK = 128
PAGE = 128
CONV_W = 4
GDN_CHUNK = 64
N_EXPERTS = 32
TOP_K = 4
SWIGLU_LIMIT = 7.0
SWIGLU_ALPHA = 1.702
EXPERT_ROWS = 256
EPS = 1e-6
NEG = -1e30
VMEM_LIMIT = 56 * 1024 * 1024

NN = (((1,), (0,)), ((), ()))
NT = (((1,), (1,)), ((), ()))


def _dot(a, b, dims=NN):
    return lax.dot_general(a, b, dims, preferred_element_type=F32)


def _split(x):
    hi = x.astype(BF16)
    lo = (x - hi.astype(F32)).astype(BF16)
    return hi, lo


def _dot3(a, b, dims=NN):
    ah, al = _split(a)
    bh, bl = _split(b)
    return _dot(ah, bh, dims) + (_dot(ah, bl, dims) + _dot(al, bh, dims))


def _dot_exact_rhs(a, b_bf16, terms=2):
    out = None
    r = a
    for _ in range(terms):
        h = r.astype(BF16)
        part = _dot(h, b_bf16)
        out = part if out is None else out + part
        r = r - h.astype(F32)
    return out


def _group_matrix(n, group, value):
    r = lax.broadcasted_iota(jnp.int32, (n, n), 0) // group
    c = lax.broadcasted_iota(jnp.int32, (n, n), 1) // group
    return jnp.where(r == c, value, 0.0).astype(BF16)


def _group_mean_matrix(n, group):
    return _group_matrix(n, group, 1.0 / group)


def _sigmoid(x):
    return 1.0 / (1.0 + jnp.exp(-x))


def _silu(x):
    return x * _sigmoid(x)


def _inproj_kernel(x_ref, g1_ref, w_ref, qg_ref, kg_ref, neg_a_ref, dtb_ref,
                   q_ref, k_ref, v_ref, gq_ref, z_ref, gb_ref):
    x = x_ref[...]
    ms = jnp.mean(x * x, axis=-1, keepdims=True)
    n = x * lax.rsqrt(ms + EPS) * g1_ref[...]
    p = _dot(n.astype(BF16), w_ref[...])
    gm = _group_mean_matrix(2 * LANES, HEAD_DIM)

    def head_norm(t, gain):
        sq = t * t
        parts = [_dot_exact_rhs(sq[:, c:c + 2 * LANES], gm) for c in range(0, WIDTH, 2 * LANES)]
        return t * lax.rsqrt(jnp.concatenate(parts, axis=1) + EPS) * gain

    q_ref[...] = head_norm(p[:, 0:WIDTH], qg_ref[...])
    k_ref[...] = head_norm(p[:, WIDTH:2 * WIDTH], kg_ref[...])
    v_ref[...] = p[:, 2 * WIDTH:3 * WIDTH]
    gq_ref[...] = p[:, OFF_GQ:OFF_Z]
    z_ref[...] = p[:, OFF_Z:OFF_AB]
    ab = p[:, OFF_AB:IN_COLS_PAD]
    t = ab + dtb_ref[...]
    softplus = jnp.maximum(t, 0.0) + jnp.log(1.0 + jnp.exp(-jnp.abs(t)))
    lane = lax.broadcasted_iota(jnp.int32, ab.shape, 1)
    gb_ref[...] = jnp.where(lane < N_HEADS, neg_a_ref[...] * softplus, _sigmoid(ab))


def _inproj(x2, g1, w_pad, qg, kg, neg_a, dtb, tm):
    t = x2.shape[0]
    row = lambda i: (i, 0)
    fixed = lambda i: (0, 0)
    outs = [jax.ShapeDtypeStruct((t, WIDTH), F32)] * 3 + [
        jax.ShapeDtypeStruct((t, GDN_QKV), F32), jax.ShapeDtypeStruct((t, WIDTH), F32),
        jax.ShapeDtypeStruct((t, LANES), F32)]
    return pl.pallas_call(
        _inproj_kernel,
        grid=(t // tm,),
        in_specs=[pl.BlockSpec((tm, D_MODEL), row), pl.BlockSpec((1, D_MODEL), fixed),
                  pl.BlockSpec((D_MODEL, IN_COLS_PAD), fixed), pl.BlockSpec((1, WIDTH), fixed),
                  pl.BlockSpec((1, WIDTH), fixed), pl.BlockSpec((1, LANES), fixed),
                  pl.BlockSpec((1, LANES), fixed)],
        out_specs=[pl.BlockSpec((tm, WIDTH), row)] * 3 + [
            pl.BlockSpec((tm, GDN_QKV), row), pl.BlockSpec((tm, WIDTH), row), pl.BlockSpec((tm, LANES), row)],
        out_shape=outs,
        compiler_params=pltpu.CompilerParams(dimension_semantics=("arbitrary",), vmem_limit_bytes=VMEM_LIMIT),
        name="inproj",
    )(x2, g1, w_pad, qg, kg, neg_a, dtb)


def _moba_prompt_kernel(slopes_ref, q_ref, k_ref, v_ref, o_ref, kmean_ref, *, n_blocks):
    hp = pl.program_id(1)
    qc = pl.program_id(2)

    @pl.when(qc == 0)
    def _():
        for n in range(n_blocks):
            kmean_ref[n:n + 1, :] = jnp.mean(k_ref[n * MOBA_BLOCK:(n + 1) * MOBA_BLOCK, :], axis=0, keepdims=True)

    ob = (qc * Q_CHUNK) // MOBA_BLOCK
    q = q_ref[...]
    kmean = kmean_ref[...]
    lane = lax.broadcasted_iota(jnp.int32, (Q_CHUNK, PAIR), 1)
    nidx = lax.broadcasted_iota(jnp.int32, (Q_CHUNK, n_blocks), 1)
    ii = lax.broadcasted_iota(jnp.int32, (Q_CHUNK, MOBA_BLOCK), 0)
    jj = lax.broadcasted_iota(jnp.int32, (Q_CHUNK, MOBA_BLOCK), 1)
    rel = ii - jj
    scale = HEAD_DIM ** -0.5
    outs = []
    for a in range(2):
        slope = slopes_ref[2 * hp + a]
        qa = jnp.where(lane // HEAD_DIM == a, q, 0.0)
        qa_b = qa.astype(BF16)
        gate = _dot3(qa, kmean, NT)
        gate = jnp.where(nidx < ob, gate, -jnp.inf)
        rank = jnp.zeros((Q_CHUNK, n_blocks), jnp.int32)
        for m in range(n_blocks):
            gm = gate[:, m:m + 1]
            rank = rank + jnp.where((gm > gate) | ((gm == gate) & (m < nidx)), 1, 0)
        sel = jnp.where((nidx < ob) & (rank < MOBA_TOPK), 1.0, 0.0)

        def block_logits(n):
            start = pl.multiple_of(n * MOBA_BLOCK, MOBA_BLOCK)
            kb = k_ref[pl.ds(start, MOBA_BLOCK), :].astype(BF16)
            vb = v_ref[pl.ds(start, MOBA_BLOCK), :].astype(BF16)
            dist = (rel + (qc * Q_CHUNK - n * MOBA_BLOCK)).astype(F32)
            return _dot(qa_b, kb, NT) * scale - slope * dist, dist, vb

        s, dist, vb = block_logits(ob)
        s = jnp.where(dist >= 0, s, NEG)
        m0 = jnp.max(s, axis=1, keepdims=True)
        p = jnp.exp(s - m0)
        l0 = jnp.sum(p, axis=1, keepdims=True)
        acc0 = _dot(p.astype(BF16), vb)

        def body(n, carry):
            m_i, l_i, acc = carry
            s, _, vb = block_logits(n)
            sel_n = jnp.sum(jnp.where(nidx == n, sel, 0.0), axis=1, keepdims=True)
            s = jnp.where(sel_n > 0, s, NEG)
            m_new = jnp.maximum(m_i, jnp.max(s, axis=1, keepdims=True))
            alpha = jnp.exp(m_i - m_new)
            p = jnp.exp(s - m_new)
            return (m_new, l_i * alpha + jnp.sum(p, axis=1, keepdims=True),
                    acc * alpha + _dot(p.astype(BF16), vb))

        _, l_f, acc_f = lax.fori_loop(0, ob, body, (m0, l0, acc0))
        outs.append(acc_f / l_f)
    o_ref[...] = jnp.where(lane < HEAD_DIM, outs[0], outs[1])


def _moba_prompt(slopes, q, k, v, batch, seq):
    n_qc = seq // Q_CHUNK
    n_blocks = seq // MOBA_BLOCK
    qmap = lambda b, hp, qc: (b * n_qc + qc, hp)
    kmap = lambda b, hp, qc: (b, hp)
    return pl.pallas_call(
        functools.partial(_moba_prompt_kernel, n_blocks=n_blocks),
        grid=(batch, N_PAIRS, n_qc),
        in_specs=[pl.BlockSpec(memory_space=pltpu.SMEM),
                  pl.BlockSpec((Q_CHUNK, PAIR), qmap), pl.BlockSpec((seq, PAIR), kmap),
                  pl.BlockSpec((seq, PAIR), kmap)],
        out_specs=pl.BlockSpec((Q_CHUNK, PAIR), qmap),
        out_shape=jax.ShapeDtypeStruct((batch * seq, WIDTH), F32),
        scratch_shapes=[pltpu.VMEM((n_blocks, PAIR), F32)],
        compiler_params=pltpu.CompilerParams(dimension_semantics=("arbitrary",) * 3, vmem_limit_bytes=VMEM_LIMIT),
        name="moba_prompt",
    )(slopes, q, k, v)


def _moba_sample_kernel(pt_ref, q_ref, kn_ref, vn_ref, k0_ref, k1_ref, v0_ref, v1_ref, o_ref,
                        q_scr, s_scr, l_scr, acc_scr, *, n_blk, n_new, past):
    ph = pl.program_id(1)
    j = pl.program_id(2)
    rows = N_HEADS * n_new
    scale = HEAD_DIM ** -0.5
    ppb = MOBA_BLOCK // PAGE

    @pl.when((ph == 0) & (j == 0))
    def _():
        for h in range(N_HEADS):
            q_scr[h] = q_ref[:, h * HEAD_DIM:(h + 1) * HEAD_DIM]

    @pl.when(ph == 0)
    def _():
        for pg, kref in enumerate((k0_ref, k1_ref)):
            col = pl.multiple_of(j * MOBA_BLOCK + pg * PAGE, PAGE)
            for h in range(N_HEADS):
                kh = kref[:, h, :].astype(BF16)
                s_scr[h * n_new:(h + 1) * n_new, pl.ds(col, PAGE)] = _dot(q_scr[h].astype(BF16), kh, NT)

    slope_col = jnp.concatenate(
        [jnp.full((n_new, 1), 2.0 ** (-8.0 * (h + 1) / N_HEADS), F32) for h in range(N_HEADS)], axis=0)
    qpos = (past + lax.broadcasted_iota(jnp.int32, (rows, 1), 0) % n_new).astype(F32)

    def block_logits(n):
        start = pl.multiple_of(n * MOBA_BLOCK, MOBA_BLOCK)
        raw = s_scr[:, pl.ds(start, MOBA_BLOCK)]
        kpos = (n * MOBA_BLOCK + lax.broadcasted_iota(jnp.int32, (rows, MOBA_BLOCK), 1)).astype(F32)
        return raw * scale - slope_col * (qpos - kpos), start

    @pl.when((ph == 1) & (j == 0))
    def _():
        lane = lax.broadcasted_iota(jnp.int32, (rows, LANES), 1)

        def gate_body(n, gate):
            start = pl.multiple_of(n * MOBA_BLOCK, MOBA_BLOCK)
            col = jnp.sum(s_scr[:, pl.ds(start, MOBA_BLOCK)], axis=1, keepdims=True) * (1.0 / MOBA_BLOCK)
            return jnp.where(lane == n, col, gate)

        gate = lax.fori_loop(0, n_blk, gate_body, jnp.full((rows, LANES), -jnp.inf, F32))
        sel = jnp.zeros((rows, LANES), F32)
        for _ in range(min(MOBA_TOPK, n_blk)):
            mx = jnp.max(gate, axis=1, keepdims=True)
            idx = jnp.min(jnp.where(gate == mx, lane, LANES), axis=1, keepdims=True)
            hit = lane == idx
            sel = jnp.where(hit, 1.0, sel)
            gate = jnp.where(hit, -jnp.inf, gate)

        own = []
        for h in range(N_HEADS):
            own.append(_dot(q_scr[h], kn_ref[:, h * HEAD_DIM:(h + 1) * HEAD_DIM], NT))
        own = jnp.concatenate(own, axis=0)
        li = lax.broadcasted_iota(jnp.int32, (rows, n_new), 0) % n_new
        ri = lax.broadcasted_iota(jnp.int32, (rows, n_new), 1)
        d_own = (li - ri).astype(F32)
        lg_own = jnp.where(d_own >= 0, own * scale - slope_col * d_own, NEG)
        m0 = jnp.max(lg_own, axis=1, keepdims=True)

        def sel_col(n):
            return jnp.sum(jnp.where(lane == n, sel, 0.0), axis=1, keepdims=True) > 0

        def max_body(n, m):
            lg, _ = block_logits(n)
            return jnp.maximum(m, jnp.max(jnp.where(sel_col(n), lg, NEG), axis=1, keepdims=True))

        m = lax.fori_loop(0, n_blk, max_body, m0)
        p_own = jnp.exp(lg_own - m)
        l0 = jnp.sum(p_own, axis=1, keepdims=True)

        def p_body(n, l):
            lg, start = block_logits(n)
            p = jnp.exp(jnp.where(sel_col(n), lg, NEG) - m)
            s_scr[:, pl.ds(start, MOBA_BLOCK)] = p
            return l + jnp.sum(p, axis=1, keepdims=True)

        l = lax.fori_loop(0, n_blk, p_body, l0)
        l_scr[...] = jnp.broadcast_to(l, (rows, LANES))
        for h in range(N_HEADS):
            acc_scr[h] = _dot(p_own[h * n_new:(h + 1) * n_new, :], vn_ref[:, h * HEAD_DIM:(h + 1) * HEAD_DIM])

    @pl.when(ph == 1)
    def _():
        for pg, vref in enumerate((v0_ref, v1_ref)):
            col = pl.multiple_of(j * MOBA_BLOCK + pg * PAGE, PAGE)
            for h in range(N_HEADS):
                vh = vref[:, h, :].astype(BF16)
                p = s_scr[h * n_new:(h + 1) * n_new, pl.ds(col, PAGE)].astype(BF16)
                acc_scr[h] = acc_scr[h] + _dot(p, vh)

    @pl.when((ph == 1) & (j == n_blk - 1))
    def _():
        for h in range(N_HEADS):
            inv = 1.0 / l_scr[h * n_new:(h + 1) * n_new, 0:1]
            o_ref[:, h * HEAD_DIM:(h + 1) * HEAD_DIM] = acc_scr[h] * inv


def _moba_sample(page_table, q, k_new, v_new, cache_k, cache_v, n_seq, n_new):
    n_pages = page_table.shape[1]
    past = n_pages * PAGE
    n_blk = past // MOBA_BLOCK
    assert n_blk * MOBA_BLOCK == past and n_new <= SUBLANES
    ppb = MOBA_BLOCK // PAGE
    tok = lambda b, ph, j, pt: (b, 0)

    def kpage(pg):
        return lambda b, ph, j, pt: (0, pt[b, ppb * jnp.where(ph == 0, j, n_blk - 1) + pg], 0, 0, 0)

    def vpage(pg):
        return lambda b, ph, j, pt: (0, pt[b, ppb * jnp.where(ph == 0, 0, j) + pg], 0, 0, 0)

    page_block = (None, None, PAGE, N_HEADS, HEAD_DIM)
    rows = N_HEADS * n_new
    return pl.pallas_call(
        functools.partial(_moba_sample_kernel, n_blk=n_blk, n_new=n_new, past=past),
        grid_spec=pltpu.PrefetchScalarGridSpec(
            num_scalar_prefetch=1,
            grid=(n_seq, 2, n_blk),
            in_specs=[pl.BlockSpec((n_new, WIDTH), tok)] * 3 + [
                pl.BlockSpec(page_block, kpage(0)), pl.BlockSpec(page_block, kpage(1)),
                pl.BlockSpec(page_block, vpage(0)), pl.BlockSpec(page_block, vpage(1))],
            out_specs=pl.BlockSpec((n_new, WIDTH), tok),
            scratch_shapes=[pltpu.VMEM((N_HEADS, n_new, HEAD_DIM), F32),
                            pltpu.VMEM((rows, past), F32),
                            pltpu.VMEM((rows, LANES), F32),
                            pltpu.VMEM((N_HEADS, n_new, HEAD_DIM), F32)]),
        out_shape=jax.ShapeDtypeStruct((n_seq * n_new, WIDTH), F32),
        compiler_params=pltpu.CompilerParams(dimension_semantics=("arbitrary",) * 3, vmem_limit_bytes=VMEM_LIMIT),
        name="moba_sample",
    )(page_table, q, k_new, v_new, cache_k, cache_k, cache_v, cache_v)


def _gdn_kernel(xq_ref, xk_ref, xv_ref, z_ref, gb_ref, cwq_ref, cwk_ref, cwv_ref, cbq_ref, cbk_ref, cbv_ref,
                s0_ref, ng_ref, o_ref, s_out_ref, xs_scr, act_scr, gate_scr, *, seq, valid, row_block):
    hp = pl.program_id(1)
    c = GDN_CHUNK
    c2 = 2 * c
    pad = SUBLANES
    lane_r = lax.broadcasted_iota(jnp.int32, (row_block, PAIR), 1)
    half_mean = _group_mean_matrix(PAIR, HEAD_DIM)
    half_sum = _group_matrix(PAIR, HEAD_DIM, 1.0)

    er = lax.broadcasted_iota(jnp.int32, (LANES, LANES), 0)
    expand = [jnp.where(er == off + 2 * hp + a, 1.0, 0.0).astype(BF16) for off in (0, N_HEADS) for a in range(2)]

    for i, (x_ref, cb_ref) in enumerate(((xq_ref, cbq_ref), (xk_ref, cbk_ref), (xv_ref, cbv_ref))):
        xs_scr[i, pad - (CONV_W - 1):pad, :] = cb_ref[...]
        xs_scr[i, pad:pad + seq, :] = x_ref[...]
    for r0 in range(0, seq, row_block):
        live = (lax.broadcasted_iota(jnp.int32, (row_block, PAIR), 0) + r0) < valid
        for i, cw_ref in enumerate((cwq_ref, cwk_ref, cwv_ref)):
            conv = None
            for t in range(CONV_W):
                start = pad - (CONV_W - 1) + t + r0
                term = xs_scr[i, start:start + row_block, :] * cw_ref[t:t + 1, :]
                conv = term if conv is None else conv + term
            act = _silu(conv)
            if i < 2:
                ss = _dot_exact_rhs(act * act, half_sum)
                act = act * lax.rsqrt(ss + EPS)
            act_scr[i, r0:r0 + row_block, :] = jnp.where(live, act, 0.0)
        gb = gb_ref[r0:r0 + row_block, :]
        for e in range(4):
            gate_scr[e, r0:r0 + row_block, :] = jnp.where(live, _dot_exact_rhs(gb, expand[e], terms=3), 0.0)

    ri = lax.broadcasted_iota(jnp.int32, (c2, c2), 0)
    ci = lax.broadcasted_iota(jnp.int32, (c2, c2), 1)
    delta = jnp.where((ri // c) == (ci // c), ri - ci, -1)
    causal = delta >= 0
    strict = delta > 0
    tril = jnp.where(causal, 1.0, 0.0).astype(BF16)
    eye = jnp.where(ri == ci, 1.0, 0.0)
    lane_c = lax.broadcasted_iota(jnp.int32, (c, PAIR), 1)
    first = lane_c < HEAD_DIM
    diag_blocks = (lax.broadcasted_iota(jnp.int32, (PAIR, PAIR), 0) // HEAD_DIM) == (
        lax.broadcasted_iota(jnp.int32, (PAIR, PAIR), 1) // HEAD_DIM)
    scale = HEAD_DIM ** -0.5

    def stack(x):
        return jnp.concatenate([jnp.where(first, x, 0.0), jnp.where(first, 0.0, x)], axis=0)

    def unstack(x):
        return x[:c] + x[c:]

    zeros = jnp.zeros((HEAD_DIM, HEAD_DIM), F32)
    s_init = jnp.concatenate([jnp.concatenate([s0_ref[0], zeros], axis=1),
                              jnp.concatenate([zeros, s0_ref[1]], axis=1)], axis=0)

    def chunk(ic, s2):
        r0 = pl.multiple_of(ic * c, c)
        qc = act_scr[0, pl.ds(r0, c), :] * scale
        kc = act_scr[1, pl.ds(r0, c), :]
        vc = act_scr[2, pl.ds(r0, c), :]
        g_st = jnp.concatenate([gate_scr[0, pl.ds(r0, c), :], gate_scr[1, pl.ds(r0, c), :]], axis=0)
        b_st = jnp.concatenate([gate_scr[2, pl.ds(r0, c), :], gate_scr[3, pl.ds(r0, c), :]], axis=0)
        gh, gl = _split(g_st)
        gl2 = (g_st - gh.astype(F32) - gl.astype(F32)).astype(BF16)
        gc_st = _dot(tril, gh) + (_dot(tril, gl) + _dot(tril, gl2))
        diff = gc_st - gc_st.T
        decay = jnp.where(causal, jnp.exp(jnp.minimum(diff, 0.0)), 0.0)
        k_st = stack(kc)
        kb_st = k_st * b_st
        k_stb = k_st.astype(BF16)
        a_mat = jnp.where(strict, _dot(kb_st.astype(BF16), k_stb, NT) * decay, 0.0)
        qk = jnp.where(causal, _dot(stack(qc).astype(BF16), k_stb, NT) * decay, 0.0)
        t_mat = eye - a_mat
        x = a_mat
        for _ in range(int(np.log2(c)) - 1):
            x = _dot3(x, x)
            t_mat = t_mat + _dot3(t_mat, x)
        rhs = jnp.concatenate([stack(vc) * b_st, kb_st * jnp.exp(gc_st)], axis=1)
        uw = _dot(t_mat.astype(BF16), rhs.astype(BF16))
        u2 = unstack(uw[:, :PAIR])
        w2 = unstack(uw[:, PAIR:])
        gc2 = jnp.where(first, gc_st[:c], gc_st[c:])
        qg2 = qc * jnp.exp(gc2)
        xs = _dot(jnp.concatenate([qg2, w2], axis=0).astype(BF16), s2.astype(BF16))
        v_new = u2 - xs[c:]
        o2 = xs[:c] + unstack(_dot(qk.astype(BF16), stack(v_new).astype(BF16)))
        g_last = gc2[c - 1:c, :]
        kd = kc * jnp.exp(g_last - gc2)
        upd = _dot(kd.T.astype(BF16), v_new.astype(BF16))
        s2 = s2 * jnp.exp(g_last) + jnp.where(diag_blocks, upd, 0.0)
        ms = _dot_exact_rhs(o2 * o2, half_mean)
        zc = z_ref[pl.ds(r0, c), :]
        o_ref[pl.ds(r0, c), :] = o2 * lax.rsqrt(ms + EPS) * ng_ref[...] * _silu(zc)
        return s2

    s_fin = lax.fori_loop(0, seq // c, chunk, s_init)
    s_out_ref[0] = s_fin[:HEAD_DIM, :HEAD_DIM]
    s_out_ref[1] = s_fin[HEAD_DIM:, HEAD_DIM:]


def _gdn(gq, z, gb, conv_w, conv_buf, s0, ng, n_seq, seq, valid):
    row_block = min(seq, 256)
    seq_map = lambda off: (lambda b, hp: (b, 0, off + hp))
    cw_map = lambda off: (lambda b, hp: (0, off + hp))
    kern = functools.partial(_gdn_kernel, seq=seq, valid=valid, row_block=row_block)
    sb = pl.BlockSpec((None, seq, PAIR), seq_map(0))
    return pl.pallas_call(
        kern,
        grid=(n_seq, N_PAIRS),
        in_specs=[pl.BlockSpec((None, seq, PAIR), seq_map(0)), pl.BlockSpec((None, seq, PAIR), seq_map(N_PAIRS)),
                  pl.BlockSpec((None, seq, PAIR), seq_map(2 * N_PAIRS)), sb,
                  pl.BlockSpec((None, seq, LANES), lambda b, hp: (b, 0, 0)),
                  pl.BlockSpec((CONV_W, PAIR), cw_map(0)), pl.BlockSpec((CONV_W, PAIR), cw_map(N_PAIRS)),
                  pl.BlockSpec((CONV_W, PAIR), cw_map(2 * N_PAIRS)),
                  pl.BlockSpec((None, CONV_W - 1, PAIR), seq_map(0)),
                  pl.BlockSpec((None, CONV_W - 1, PAIR), seq_map(N_PAIRS)),
                  pl.BlockSpec((None, CONV_W - 1, PAIR), seq_map(2 * N_PAIRS)),
                  pl.BlockSpec((None, 2, HEAD_DIM, HEAD_DIM), lambda b, hp: (b, hp, 0, 0)),
                  pl.BlockSpec((1, PAIR), lambda b, hp: (0, 0))],
        out_specs=[sb, pl.BlockSpec((None, 2, HEAD_DIM, HEAD_DIM), lambda b, hp: (b, hp, 0, 0))],
        out_shape=[jax.ShapeDtypeStruct((n_seq, seq, WIDTH), F32),
                   jax.ShapeDtypeStruct((n_seq, N_HEADS, HEAD_DIM, HEAD_DIM), F32)],
        scratch_shapes=[pltpu.VMEM((3, seq + SUBLANES, PAIR), F32), pltpu.VMEM((3, seq, PAIR), F32),
                        pltpu.VMEM((4, seq, PAIR), F32)],
        compiler_params=pltpu.CompilerParams(dimension_semantics=("arbitrary",) * 2, vmem_limit_bytes=VMEM_LIMIT),
        name="gdn",
    )(gq, gq, gq, z, gb, conv_w, conv_w, conv_w, conv_buf, conv_buf, conv_buf, s0, ng)


def _outproj_kernel(x_ref, att_ref, gdn_ref, wo_ref, g2_ref, wr_ref, br_ref, h_ref, m_ref, ti_ref, tg_ref):
    mix = jnp.concatenate([att_ref[...], gdn_ref[...]], axis=1).astype(BF16)
    h = x_ref[...] + _dot(mix, wo_ref[...])
    h_ref[...] = h
    ms = jnp.mean(h * h, axis=-1, keepdims=True)
    m = h * lax.rsqrt(ms + EPS) * g2_ref[...]
    for s in range(D_MODEL // LANES):
        m_ref[:, s, :] = m[:, s * LANES:(s + 1) * LANES]
    logits = _dot3(m, wr_ref[...]) + br_ref[...]
    lane = lax.broadcasted_iota(jnp.int32, logits.shape, 1)
    ti = jnp.zeros(logits.shape, jnp.int32)
    tv = jnp.full(logits.shape, NEG, F32)
    for r in range(TOP_K):
        mx = jnp.max(logits, axis=1, keepdims=True)
        idx = jnp.min(jnp.where(logits == mx, lane, LANES), axis=1, keepdims=True)
        ti = jnp.where(lane == r, idx, ti)
        tv = jnp.where(lane == r, mx, tv)
        logits = jnp.where(lane == idx, -jnp.inf, logits)
    e = jnp.exp(tv - jnp.max(tv, axis=1, keepdims=True))
    ti_ref[...] = ti
    tg_ref[...] = e / jnp.sum(e, axis=1, keepdims=True)


def _outproj(x2, att, gdn, wo, g2, wr_pad, br_pad, tm):
    t = x2.shape[0]
    row = lambda i: (i, 0)
    fixed = lambda i: (0, 0)
    return pl.pallas_call(
        _outproj_kernel,
        grid=(t // tm,),
        in_specs=[pl.BlockSpec((tm, D_MODEL), row), pl.BlockSpec((tm, WIDTH), row), pl.BlockSpec((tm, WIDTH), row),
                  pl.BlockSpec((D_MODEL, D_MODEL), fixed), pl.BlockSpec((1, D_MODEL), fixed),
                  pl.BlockSpec((D_MODEL, LANES), fixed), pl.BlockSpec((1, LANES), fixed)],
        out_specs=[pl.BlockSpec((tm, D_MODEL), row),
                   pl.BlockSpec((tm, D_MODEL // LANES, LANES), lambda i: (i, 0, 0)),
                   pl.BlockSpec((tm, LANES), row), pl.BlockSpec((tm, LANES), row)],
        out_shape=[jax.ShapeDtypeStruct((t, D_MODEL), F32),
                   jax.ShapeDtypeStruct((t, D_MODEL // LANES, LANES), F32),
                   jax.ShapeDtypeStruct((t, LANES), jnp.int32), jax.ShapeDtypeStruct((t, LANES), F32)],
        compiler_params=pltpu.CompilerParams(dimension_semantics=("arbitrary",), vmem_limit_bytes=VMEM_LIMIT),
        name="outproj",
    )(x2, att, gdn, wo, g2, wr_pad, br_pad)


def _dispatch_kernel(dest_ref, m_ref, xin_ref, xbuf_ref, sem, *, tb):
    del xin_ref
    base = pl.program_id(0) * tb

    def row_copy(t, k):
        return pltpu.make_async_copy(m_ref.at[base + t], xbuf_ref.at[dest_ref[t * TOP_K + k]], sem)

    def issue(t, _):
        for k in range(TOP_K):
            row_copy(t, k).start()
        return 0

    lax.fori_loop(0, tb, issue, 0)

    def drain(t, _):
        for k in range(TOP_K):
            row_copy(t, k).wait()
        return 0

    lax.fori_loop(0, tb, drain, 0)


def _dispatch(dest_flat, m3, xbuf, tb):
    t = m3.shape[0]
    return pl.pallas_call(
        functools.partial(_dispatch_kernel, tb=tb),
        grid=(t // tb,),
        in_specs=[pl.BlockSpec((tb * TOP_K,), lambda i: (i,), memory_space=pltpu.SMEM),
                  pl.BlockSpec(memory_space=pl.ANY), pl.BlockSpec(memory_space=pl.ANY)],
        out_specs=pl.BlockSpec(memory_space=pl.ANY),
        out_shape=jax.ShapeDtypeStruct(xbuf.shape, xbuf.dtype),
        scratch_shapes=[pltpu.SemaphoreType.DMA(())],
        input_output_aliases={2: 0},
        compiler_params=pltpu.CompilerParams(dimension_semantics=("arbitrary",), has_side_effects=True),
        name="moe_dispatch",
    )(dest_flat, m3, xbuf)


def _experts_kernel(be_ref, nb_ref, x_ref, wg_ref, wl_ref, wd_ref, bg_ref, bl_ref, bd_ref, y_ref):
    i = pl.program_id(0)
    n_s = D_MODEL // LANES

    @pl.when(i < nb_ref[0])
    def _():
        x = jnp.concatenate([x_ref[:, s, :] for s in range(n_s)], axis=1).astype(BF16)
        gate = jnp.minimum(_dot(x, wg_ref[...]) + bg_ref[...], SWIGLU_LIMIT)
        lin = jnp.clip(_dot(x, wl_ref[...]) + bl_ref[...], -SWIGLU_LIMIT, SWIGLU_LIMIT)
        hmid = gate * _sigmoid(SWIGLU_ALPHA * gate) * (lin + 1.0)
        y = _dot(hmid.astype(BF16), wd_ref[...]) + bd_ref[...]
        for s in range(n_s):
            y_ref[:, s, :] = y[:, s * LANES:(s + 1) * LANES]

    @pl.when(i >= nb_ref[0])
    def _():
        y_ref[...] = jnp.zeros(y_ref.shape, y_ref.dtype)


def _experts(block_e, nb_used, xbuf, wg, wl, wd, bg, bl, bd):
    n_rows = xbuf.shape[0]
    n_s = D_MODEL // LANES
    d_ff = wg.shape[2]
    xmap = lambda i, be, nb: (jnp.minimum(i, nb[0] - 1), 0, 0)
    wmap = lambda i, be, nb: (be[i], 0, 0)
    return pl.pallas_call(
        _experts_kernel,
        grid_spec=pltpu.PrefetchScalarGridSpec(
            num_scalar_prefetch=2,
            grid=(n_rows // EXPERT_ROWS,),
            in_specs=[pl.BlockSpec((EXPERT_ROWS, n_s, LANES), xmap),
                      pl.BlockSpec((None, D_MODEL, d_ff), wmap), pl.BlockSpec((None, D_MODEL, d_ff), wmap),
                      pl.BlockSpec((None, d_ff, D_MODEL), wmap),
                      pl.BlockSpec((None, 1, d_ff), wmap), pl.BlockSpec((None, 1, d_ff), wmap),
                      pl.BlockSpec((None, 1, D_MODEL), wmap)],
            out_specs=pl.BlockSpec((EXPERT_ROWS, n_s, LANES), lambda i, be, nb: (i, 0, 0))),
        out_shape=jax.ShapeDtypeStruct(xbuf.shape, F32),
        compiler_params=pltpu.CompilerParams(dimension_semantics=("arbitrary",), vmem_limit_bytes=VMEM_LIMIT),
        name="moe_experts",
    )(block_e, nb_used, xbuf, wg, wl, wd, bg, bl, bd)


def _combine_kernel(dest_ref, h_ref, tg_ref, y_ref, o_ref, buf, sem, *, tb):
    n_s = D_MODEL // LANES

    def row_copy(t, k):
        return pltpu.make_async_copy(y_ref.at[dest_ref[t * TOP_K + k]], buf.at[k, t], sem)

    def issue(t, _):
        for k in range(TOP_K):
            row_copy(t, k).start()
        return 0

    lax.fori_loop(0, tb, issue, 0)

    def drain(t, _):
        for k in range(TOP_K):
            row_copy(t, k).wait()
        return 0

    lax.fori_loop(0, tb, drain, 0)
    tg = tg_ref[...]
    for s in range(n_s):
        acc = h_ref[:, s * LANES:(s + 1) * LANES]
        for k in range(TOP_K):
            acc = acc + tg[:, k:k + 1] * buf[k, :, s, :]
        o_ref[:, s * LANES:(s + 1) * LANES] = acc


def _combine(dest_flat, h, tg, ybuf, tb):
    t = h.shape[0]
    n_s = D_MODEL // LANES
    row = lambda i: (i, 0)
    return pl.pallas_call(
        functools.partial(_combine_kernel, tb=tb),
        grid=(t // tb,),
        in_specs=[pl.BlockSpec((tb * TOP_K,), lambda i: (i,), memory_space=pltpu.SMEM),
                  pl.BlockSpec((tb, D_MODEL), row), pl.BlockSpec((tb, LANES), row),
                  pl.BlockSpec(memory_space=pl.ANY)],
        out_specs=pl.BlockSpec((tb, D_MODEL), row),
        out_shape=jax.ShapeDtypeStruct((t, D_MODEL), F32),
        scratch_shapes=[pltpu.VMEM((TOP_K, tb, n_s, LANES), F32), pltpu.SemaphoreType.DMA(())],
        compiler_params=pltpu.CompilerParams(dimension_semantics=("arbitrary",), vmem_limit_bytes=VMEM_LIMIT),
        name="moe_combine",
    )(dest_flat, h, tg, ybuf)


def _routing(flat_e, n_blocks):
    onehot = (flat_e[:, None] == jnp.arange(N_EXPERTS, dtype=jnp.int32)[None, :]).astype(jnp.int32)
    csum = jnp.cumsum(onehot, axis=0)
    rank = jnp.sum(onehot * csum, axis=1) - 1
    counts = csum[-1]
    padded = (counts + EXPERT_ROWS - 1) // EXPERT_ROWS * EXPERT_ROWS
    pad_end = jnp.cumsum(padded)
    pad_start = pad_end - padded
    dest = (pad_start[flat_e] + rank).astype(jnp.int32)
    block_e = jnp.minimum(
        jnp.searchsorted(pad_end, jnp.arange(n_blocks, dtype=jnp.int32) * EXPERT_ROWS, side='right'),
        N_EXPERTS - 1).astype(jnp.int32)
    nb_used = (pad_end[-1:] // EXPERT_ROWS).astype(jnp.int32)
    return dest, block_e, nb_used


def kernel(x_prompt, x_sample, cache_k, cache_v, page_table, state_gdn, state_conv, norm1_g, w_in, q_norm_g,
           k_norm_g, conv_w, a_log, dt_bias, gdn_norm_g, w_out, norm2_g, w_router, b_router, w_gate_up,
           b_gate_up, w_down, b_down):
    assert w_in.shape[0] == 1, "single layer"
    b, s, d = x_prompt.shape
    db, dl, _ = x_sample.shape
    tp, ts = b * s, db * dl

    w_in_p = jnp.pad(w_in[0], ((0, 0), (0, IN_COLS_PAD - IN_COLS))).astype(BF16)
    qg = jnp.tile(q_norm_g[0], N_HEADS)[None, :]
    kg = jnp.tile(k_norm_g[0], N_HEADS)[None, :]
    neg_a = jnp.pad(-jnp.exp(a_log[0]), (0, LANES - N_HEADS))[None, :]
    dtb = jnp.pad(dt_bias[0], (0, LANES - N_HEADS))[None, :]
    ng = jnp.tile(gdn_norm_g[0], 2)[None, :]
    wo = w_out[0].astype(BF16)
    wr = jnp.pad(w_router[0], ((0, 0), (0, LANES - N_EXPERTS)))
    br = jnp.pad(b_router[0], (0, LANES - N_EXPERTS), constant_values=NEG)[None, :]
    wg = w_gate_up[0][:, :, 0::2].astype(BF16)
    wl = w_gate_up[0][:, :, 1::2].astype(BF16)
    bg = b_gate_up[0][:, None, 0::2]
    bl = b_gate_up[0][:, None, 1::2]
    wd = w_down[0].astype(BF16)
    bd = b_down[0][:, None, :]
    slopes = (2.0 ** (-8.0 * jnp.arange(1, N_HEADS + 1, dtype=F32) / N_HEADS)).astype(F32)

    xp2 = x_prompt.reshape(tp, d)
    xs2 = x_sample.reshape(ts, d)
    g1 = norm1_g[0][None, :]
    g2 = norm2_g[0][None, :]

    qp, kp, vp, gqp, zp, gbp = _inproj(xp2, g1, w_in_p, qg, kg, neg_a, dtb, tm=512)
    att_p = _moba_prompt(slopes, qp, kp, vp, b, s)
    gdn_p, gstate_p = _gdn(gqp.reshape(b, s, GDN_QKV), zp.reshape(b, s, WIDTH), gbp.reshape(b, s, LANES),
                           conv_w[0], jnp.zeros((b, CONV_W - 1, GDN_QKV), F32),
                           jnp.zeros((b, N_HEADS, HEAD_DIM, HEAD_DIM), F32), ng, b, s, s)

    qs, ks, vs, gqs, zs, gbs = _inproj(xs2, g1, w_in_p, qg, kg, neg_a, dtb, tm=ts)
    att_s = _moba_sample(page_table, qs, ks, vs, cache_k, cache_v, db, dl)
    padl = lambda a: jnp.pad(a.reshape(db, dl, -1), ((0, 0), (0, GDN_CHUNK - dl), (0, 0)))
    gdn_s, gstate_s = _gdn(padl(gqs), padl(zs), padl(gbs), conv_w[0], state_conv[0], state_gdn[0], ng,
                           db, GDN_CHUNK, dl)
    gdn_s = gdn_s[:, :dl].reshape(ts, WIDTH)

    hp, mp, tip, tgp = _outproj(xp2, att_p, gdn_p.reshape(tp, WIDTH), wo, g2, wr, br, tm=512)
    hs, msm, tis, tgs = _outproj(xs2, att_s, gdn_s, wo, g2, wr, br, tm=ts)

    n_assign = (tp + ts) * TOP_K
    n_blocks = (n_assign + N_EXPERTS * (EXPERT_ROWS - 1)) // EXPERT_ROWS + 1
    flat_e = jnp.concatenate([tip[:, :TOP_K].reshape(-1), tis[:, :TOP_K].reshape(-1)])
    dest, block_e, nb_used = _routing(flat_e, n_blocks)
    dest_p, dest_s = dest[:tp * TOP_K], dest[tp * TOP_K:]
    xbuf = jnp.zeros((n_blocks * EXPERT_ROWS, D_MODEL // LANES, LANES), F32)
    xbuf = _dispatch(dest_p, mp, xbuf, tb=256)
    xbuf = _dispatch(dest_s, msm, xbuf, tb=ts)
    ybuf = _experts(block_e, nb_used, xbuf, wg, wl, wd, bg, bl, bd)
    y_prompt = _combine(dest_p, hp, tgp, ybuf, tb=256).reshape(b, s, d)
    y_sample = _combine(dest_s, hs, tgs, ybuf, tb=ts).reshape(db, dl, d)

    k_prompt = kp.reshape(1, b, s // PAGE, PAGE, N_HEADS, HEAD_DIM)
    v_prompt = vp.reshape(1, b, s // PAGE, PAGE, N_HEADS, HEAD_DIM)
    k_sample = ks.reshape(1, db, dl, N_HEADS, HEAD_DIM)
    v_sample = vs.reshape(1, db, dl, N_HEADS, HEAD_DIM)
    conv_prompt = gqp.reshape(b, s, GDN_QKV)[:, s - (CONV_W - 1):][None]
    conv_sample = jnp.concatenate([state_conv[0], gqs.reshape(db, dl, GDN_QKV)], axis=1)[:, -(CONV_W - 1):][None]
    return (y_prompt, y_sample, k_prompt, v_prompt, k_sample, v_sample,
            gstate_p[None], conv_prompt, gstate_s[None], conv_sample)
```

```python
import functools

import jax
import jax.numpy as jnp
import numpy as np
from jax import lax
from jax.experimental import pallas as pl
from jax.experimental.pallas import tpu as pltpu

F32 = jnp.float32
BF16 = jnp.bfloat16

LANES = 128
SUBLANES = 8
HEAD_DIM = 64
N_HEADS = 8
PAIR = 2 * HEAD_DIM
N_PAIRS = N_HEADS // 2
WIDTH = N_HEADS * HEAD_DIM
D_MODEL = 1024
GDN_QKV = 3 * WIDTH
IN_COLS = 3 * WIDTH + GDN_QKV + WIDTH + 2 * N_HEADS
IN_COLS_PAD = 3712
OFF_GQ = 3 * WIDTH
OFF_Z = OFF_GQ + GDN_QKV
OFF_AB = OFF_Z + WIDTH
MOBA_BLOCK = 256
MOBA_TOPK = 3
Q_CHUNK = 128
PAGE = 128
CONV_W = 4
GDN_CHUNK = 64
N_EXPERTS = 32
TOP_K = 4
SWIGLU_LIMIT = 7.0
SWIGLU_ALPHA = 1.702
EXPERT_ROWS = 256
SAMPLE_PAGES_PER_STEP = 8
EPS = 1e-6
NEG = -1e30
VMEM_LIMIT = 56 * 1024 * 1024

NN = (((1,), (0,)), ((), ()))
NT = (((1,), (1,)), ((), ()))


def _dot(a, b, dims=NN):
    return lax.dot_general(a, b, dims, preferred_element_type=F32)


def _split(x):
    hi = x.astype(BF16)
    lo = (x - hi.astype(F32)).astype(BF16)
    return hi, lo


def _dot3(a, b, dims=NN):
    ah, al = _split(a)
    bh, bl = _split(b)
    return _dot(ah, bh, dims) + (_dot(ah, bl, dims) + _dot(al, bh, dims))


def _dot_exact_rhs(a, b_bf16, terms=2):
    out = None
    r = a
    for _ in range(terms):
        h = r.astype(BF16)
        part = _dot(h, b_bf16)
        out = part if out is None else out + part
        r = r - h.astype(F32)
    return out


def _group_matrix(n, group, value):
    r = lax.broadcasted_iota(jnp.int32, (n, n), 0) // group
    c = lax.broadcasted_iota(jnp.int32, (n, n), 1) // group
    return jnp.where(r == c, value, 0.0).astype(BF16)


def _group_mean_matrix(n, group):
    return _group_matrix(n, group, 1.0 / group)


def _sigmoid(x):
    return 1.0 / (1.0 + jnp.exp(-x))


def _silu(x):
    return x * _sigmoid(x)


def _inproj_kernel(x_ref, g1_ref, w_ref, qg_ref, kg_ref, neg_a_ref, dtb_ref,
                   q_ref, k_ref, v_ref, gq_ref, z_ref, gb_ref):
    x = x_ref[...]
    ms = jnp.mean(x * x, axis=-1, keepdims=True)
    n = x * lax.rsqrt(ms + EPS) * g1_ref[...]
    p = _dot(n.astype(BF16), w_ref[...])
    gm = _group_mean_matrix(2 * LANES, HEAD_DIM)

    def head_norm(t, gain):
        sq = t * t
        parts = [_dot_exact_rhs(sq[:, c:c + 2 * LANES], gm) for c in range(0, WIDTH, 2 * LANES)]
        return t * lax.rsqrt(jnp.concatenate(parts, axis=1) + EPS) * gain

    q_ref[...] = head_norm(p[:, 0:WIDTH], qg_ref[...])
    k_ref[...] = head_norm(p[:, WIDTH:2 * WIDTH], kg_ref[...])
    v_ref[...] = p[:, 2 * WIDTH:3 * WIDTH]
    gq_ref[...] = p[:, OFF_GQ:OFF_Z]
    z_ref[...] = p[:, OFF_Z:OFF_AB]
    ab = p[:, OFF_AB:IN_COLS_PAD]
    t = ab + dtb_ref[...]
    softplus = jnp.maximum(t, 0.0) + jnp.log(1.0 + jnp.exp(-jnp.abs(t)))
    lane = lax.broadcasted_iota(jnp.int32, ab.shape, 1)
    gb_ref[...] = jnp.where(lane < N_HEADS, neg_a_ref[...] * softplus, _sigmoid(ab))


def _inproj(x2, g1, w_pad, qg, kg, neg_a, dtb, tm):
    t = x2.shape[0]
    row = lambda i: (i, 0)
    fixed = lambda i: (0, 0)
    outs = [jax.ShapeDtypeStruct((t, WIDTH), F32)] * 3 + [
        jax.ShapeDtypeStruct((t, GDN_QKV), F32), jax.ShapeDtypeStruct((t, WIDTH), F32),
        jax.ShapeDtypeStruct((t, LANES), F32)]
    return pl.pallas_call(
        _inproj_kernel,
        grid=(t // tm,),
        in_specs=[pl.BlockSpec((tm, D_MODEL), row), pl.BlockSpec((1, D_MODEL), fixed),
                  pl.BlockSpec((D_MODEL, IN_COLS_PAD), fixed), pl.BlockSpec((1, WIDTH), fixed),
                  pl.BlockSpec((1, WIDTH), fixed), pl.BlockSpec((1, LANES), fixed),
                  pl.BlockSpec((1, LANES), fixed)],
        out_specs=[pl.BlockSpec((tm, WIDTH), row)] * 3 + [
            pl.BlockSpec((tm, GDN_QKV), row), pl.BlockSpec((tm, WIDTH), row), pl.BlockSpec((tm, LANES), row)],
        out_shape=outs,
        compiler_params=pltpu.CompilerParams(dimension_semantics=("arbitrary",), vmem_limit_bytes=VMEM_LIMIT),
        name="inproj",
    )(x2, g1, w_pad, qg, kg, neg_a, dtb)


def _moba_prompt_kernel(slopes_ref, q_ref, k_ref, v_ref, o_ref, kmean_ref, vt_ref, acc_ref, *, n_blocks):
    hp = pl.program_id(1)
    ob = pl.program_id(2)

    @pl.when(ob == 0)
    def _():
        for n in range(n_blocks):
            blk = slice(n * MOBA_BLOCK, (n + 1) * MOBA_BLOCK)
            kmean_ref[n:n + 1, :] = jnp.mean(k_ref[blk, :], axis=0, keepdims=True)
            vt_ref[:, blk] = v_ref[blk, :].T.astype(BF16)

    bq = MOBA_BLOCK
    cols = 2 * bq
    q = q_ref[...]
    lane = lax.broadcasted_iota(jnp.int32, (bq, PAIR), 1)
    qst = jnp.concatenate([jnp.where(lane < HEAD_DIM, q, 0.0), jnp.where(lane < HEAD_DIM, 0.0, q)], axis=0)
    qst_b = (qst * HEAD_DIM ** -0.5).astype(BF16)
    bidx = lax.broadcasted_iota(jnp.int32, (n_blocks, cols), 0)
    gate = jnp.where(bidx < ob, _dot3(kmean_ref[...], qst, NT), -jnp.inf)
    valid = jnp.where(bidx < ob, 1.0, 0.0)
    sel = jnp.zeros((n_blocks, cols), F32)
    for _ in range(min(MOBA_TOPK, n_blocks - 1)):
        mx = jnp.max(gate, axis=0, keepdims=True)
        idx = jnp.min(jnp.where(gate == mx, bidx, n_blocks), axis=0, keepdims=True)
        hit = bidx == idx
        sel = jnp.where(hit, valid, sel)
        gate = jnp.where(hit, -jnp.inf, gate)

    col = lax.broadcasted_iota(jnp.int32, (1, cols), 1)
    slope = jnp.where(col < bq, slopes_ref[2 * hp], slopes_ref[2 * hp + 1])
    rel = (lax.broadcasted_iota(jnp.int32, (bq, cols), 1) % bq
           - lax.broadcasted_iota(jnp.int32, (bq, cols), 0)).astype(F32)
    alibi = slope * rel

    def logits(n):
        start = pl.multiple_of(n * MOBA_BLOCK, MOBA_BLOCK)
        kb = k_ref[pl.ds(start, MOBA_BLOCK), :].astype(BF16)
        return _dot(kb, qst_b, NT) - alibi, start

    s, start = logits(ob)
    s = jnp.where(rel >= 0, s, NEG)
    m0 = jnp.max(s, axis=0, keepdims=True)
    p = jnp.exp(s - m0)
    l0 = jnp.sum(p, axis=0, keepdims=True)
    acc_ref[...] = _dot(vt_ref[:, pl.ds(start, MOBA_BLOCK)], p.astype(BF16))

    def body(n, carry):
        m_i, l_i = carry
        s, start = logits(n)
        sel_n = jnp.sum(jnp.where(bidx == n, sel, 0.0), axis=0, keepdims=True)
        far = slope * jnp.full((1, cols), (ob - n) * MOBA_BLOCK, jnp.int32).astype(F32)
        s = jnp.where(sel_n > 0, s - far, NEG)
        m_new = jnp.maximum(m_i, jnp.max(s, axis=0, keepdims=True))
        alpha = jnp.exp(m_i - m_new)
        p = jnp.exp(s - m_new)
        acc_ref[...] = acc_ref[...] * alpha + _dot(vt_ref[:, pl.ds(start, MOBA_BLOCK)], p.astype(BF16))
        return m_new, l_i * alpha + jnp.sum(p, axis=0, keepdims=True)

    _, l_f = lax.fori_loop(0, ob, body, (m0, l0))
    o_t = acc_ref[...] * (1.0 / l_f)
    o_ref[...] = jnp.concatenate([o_t[:HEAD_DIM, :bq], o_t[HEAD_DIM:, bq:]], axis=0).T


def _moba_prompt(slopes, q, k, v, batch, seq):
    n_blocks = seq // MOBA_BLOCK
    qmap = lambda b, hp, ob: (b * n_blocks + ob, hp)
    kmap = lambda b, hp, ob: (b, hp)
    return pl.pallas_call(
        functools.partial(_moba_prompt_kernel, n_blocks=n_blocks),
        grid=(batch, N_PAIRS, n_blocks),
        in_specs=[pl.BlockSpec(memory_space=pltpu.SMEM),
                  pl.BlockSpec((MOBA_BLOCK, PAIR), qmap), pl.BlockSpec((seq, PAIR), kmap),
                  pl.BlockSpec((seq, PAIR), kmap)],
        out_specs=pl.BlockSpec((MOBA_BLOCK, PAIR), qmap),
        out_shape=jax.ShapeDtypeStruct((batch * seq, WIDTH), F32),
        scratch_shapes=[pltpu.VMEM((n_blocks, PAIR), F32), pltpu.VMEM((PAIR, seq), BF16),
                        pltpu.VMEM((PAIR, 2 * MOBA_BLOCK), F32)],
        compiler_params=pltpu.CompilerParams(dimension_semantics=("arbitrary",) * 3, vmem_limit_bytes=VMEM_LIMIT),
        name="moba_prompt",
    )(slopes, q, k, v)


def _moba_sample_kernel(pt_ref, q_ref, kn_ref, vn_ref, *refs, n_blk, n_new, past, pps):
    k_refs, v_refs = refs[:pps], refs[pps:2 * pps]
    o_ref, q_scr, s_scr, l_scr, acc_scr = refs[2 * pps:]
    ph = pl.program_id(1)
    j = pl.program_id(2)
    rows = N_HEADS * n_new
    scale = HEAD_DIM ** -0.5
    n_steps = (past // PAGE) // pps
    head_of_lane = lax.broadcasted_iota(jnp.int32, (n_new, WIDTH), 1) // HEAD_DIM

    @pl.when((ph == 0) & (j == 0))
    def _():
        q = q_ref[...]
        for h in range(N_HEADS):
            q_scr[h * n_new:(h + 1) * n_new, :] = jnp.where(head_of_lane == h, q, 0.0)

    @pl.when(ph == 0)
    def _():
        qb = q_scr[...].astype(BF16)
        for pg in range(pps):
            col = pl.multiple_of((j * pps + pg) * PAGE, PAGE)
            s_scr[:, pl.ds(col, PAGE)] = _dot(qb, k_refs[pg][...].astype(BF16))

    slope_col = jnp.concatenate(
        [jnp.full((n_new, 1), 2.0 ** (-8.0 * (h + 1) / N_HEADS), F32) for h in range(N_HEADS)], axis=0)
    qpos = (past + lax.broadcasted_iota(jnp.int32, (rows, 1), 0) % n_new).astype(F32)

    def block_logits(n):
        start = pl.multiple_of(n * MOBA_BLOCK, MOBA_BLOCK)
        raw = s_scr[:, pl.ds(start, MOBA_BLOCK)]
        kpos = (n * MOBA_BLOCK + lax.broadcasted_iota(jnp.int32, (rows, MOBA_BLOCK), 1)).astype(F32)
        return raw * scale - slope_col * (qpos - kpos), start

    @pl.when((ph == 1) & (j == 0))
    def _():
        lane = lax.broadcasted_iota(jnp.int32, (rows, LANES), 1)

        def gate_body(n, gate):
            start = pl.multiple_of(n * MOBA_BLOCK, MOBA_BLOCK)
            col = jnp.sum(s_scr[:, pl.ds(start, MOBA_BLOCK)], axis=1, keepdims=True) * (1.0 / MOBA_BLOCK)
            return jnp.where(lane == n, col, gate)

        gate = lax.fori_loop(0, n_blk, gate_body, jnp.full((rows, LANES), -jnp.inf, F32))
        sel = jnp.zeros((rows, LANES), F32)
        for _ in range(min(MOBA_TOPK, n_blk)):
            mx = jnp.max(gate, axis=1, keepdims=True)
            idx = jnp.min(jnp.where(gate == mx, lane, LANES), axis=1, keepdims=True)
            hit = lane == idx
            sel = jnp.where(hit, 1.0, sel)
            gate = jnp.where(hit, -jnp.inf, gate)

        own = _dot(q_scr[...], kn_ref[...], NT)
        li = lax.broadcasted_iota(jnp.int32, (rows, n_new), 0) % n_new
        ri = lax.broadcasted_iota(jnp.int32, (rows, n_new), 1)
        d_own = (li - ri).astype(F32)
        lg_own = jnp.where(d_own >= 0, own * scale - slope_col * d_own, NEG)
        m0 = jnp.max(lg_own, axis=1, keepdims=True)

        def sel_col(n):
            return jnp.sum(jnp.where(lane == n, sel, 0.0), axis=1, keepdims=True) > 0

        def max_body(n, m):
            lg, _ = block_logits(n)
            return jnp.maximum(m, jnp.max(jnp.where(sel_col(n), lg, NEG), axis=1, keepdims=True))

        m = lax.fori_loop(0, n_blk, max_body, m0)
        p_own = jnp.exp(lg_own - m)
        l0 = jnp.sum(p_own, axis=1, keepdims=True)

        def p_body(n, l):
            lg, start = block_logits(n)
            p = jnp.exp(jnp.where(sel_col(n), lg, NEG) - m)
            s_scr[:, pl.ds(start, MOBA_BLOCK)] = p
            return l + jnp.sum(p, axis=1, keepdims=True)

        l = lax.fori_loop(0, n_blk, p_body, l0)
        l_scr[...] = jnp.broadcast_to(l, (rows, LANES))
        acc_scr[...] = _dot(p_own, vn_ref[...])

    @pl.when(ph == 1)
    def _():
        acc = acc_scr[...]
        for pg in range(pps):
            col = pl.multiple_of((j * pps + pg) * PAGE, PAGE)
            acc = acc + _dot(s_scr[:, pl.ds(col, PAGE)].astype(BF16), v_refs[pg][...].astype(BF16), NT)
        acc_scr[...] = acc

    @pl.when((ph == 1) & (j == n_steps - 1))
    def _():
        out = jnp.zeros((n_new, WIDTH), F32)
        for h in range(N_HEADS):
            rs = slice(h * n_new, (h + 1) * n_new)
            out = out + jnp.where(head_of_lane == h, acc_scr[rs, :] * (1.0 / l_scr[rs, 0:1]), 0.0)
        o_ref[...] = out


def _moba_sample(page_table, q, k_new, v_new, cache_kt, cache_vt, n_seq, n_new):
    n_pages = page_table.shape[1]
    past = n_pages * PAGE
    n_blk = past // MOBA_BLOCK
    pps = min(SAMPLE_PAGES_PER_STEP, n_pages)
    n_steps = n_pages // pps
    assert n_blk * MOBA_BLOCK == past and n_steps * pps == n_pages and n_new <= SUBLANES
    tok = lambda b, ph, j, pt: (b, 0)

    def kpage(pg):
        return lambda b, ph, j, pt: (pt[b, pps * jnp.where(ph == 0, j, n_steps - 1) + pg], 0, 0)

    def vpage(pg):
        return lambda b, ph, j, pt: (pt[b, pps * jnp.where(ph == 0, 0, j) + pg], 0, 0)

    page_block = (None, WIDTH, PAGE)
    rows = N_HEADS * n_new
    return pl.pallas_call(
        functools.partial(_moba_sample_kernel, n_blk=n_blk, n_new=n_new, past=past, pps=pps),
        grid_spec=pltpu.PrefetchScalarGridSpec(
            num_scalar_prefetch=1,
            grid=(n_seq, 2, n_steps),
            in_specs=[pl.BlockSpec((n_new, WIDTH), tok)] * 3
            + [pl.BlockSpec(page_block, kpage(pg)) for pg in range(pps)]
            + [pl.BlockSpec(page_block, vpage(pg)) for pg in range(pps)],
            out_specs=pl.BlockSpec((n_new, WIDTH), tok),
            scratch_shapes=[pltpu.VMEM((rows, WIDTH), F32),
                            pltpu.VMEM((rows, past), F32),
                            pltpu.VMEM((rows, LANES), F32),
                            pltpu.VMEM((rows, WIDTH), F32)]),
        out_shape=jax.ShapeDtypeStruct((n_seq * n_new, WIDTH), F32),
        compiler_params=pltpu.CompilerParams(dimension_semantics=("arbitrary",) * 3, vmem_limit_bytes=VMEM_LIMIT),
        name="moba_sample",
    )(page_table, q, k_new, v_new, *([cache_kt] * pps), *([cache_vt] * pps))


def _gdn_kernel(xq_ref, xk_ref, xv_ref, z_ref, gb_ref, cwq_ref, cwk_ref, cwv_ref, cbq_ref, cbk_ref, cbv_ref,
                s0_ref, ng_ref, o_ref, s_out_ref, xs_scr, act_scr, gate_scr, u_scr, qgw_scr, kdt_scr, qk_scr,
                dec_scr, *, seq, valid, row_block):
    hp = pl.program_id(1)
    c = GDN_CHUNK
    c2 = 2 * c
    pad = SUBLANES
    lane_r = lax.broadcasted_iota(jnp.int32, (row_block, PAIR), 1)
    half_mean = _group_mean_matrix(PAIR, HEAD_DIM)
    half_sum = _group_matrix(PAIR, HEAD_DIM, 1.0)

    er = lax.broadcasted_iota(jnp.int32, (LANES, LANES), 0)
    expand = [jnp.where(er == off + 2 * hp + a, 1.0, 0.0).astype(BF16) for off in (0, N_HEADS) for a in range(2)]

    for i, (x_ref, cb_ref) in enumerate(((xq_ref, cbq_ref), (xk_ref, cbk_ref), (xv_ref, cbv_ref))):
        xs_scr[i, pad - (CONV_W - 1):pad, :] = cb_ref[...]
        xs_scr[i, pad:pad + seq, :] = x_ref[...]
    for r0 in range(0, seq, row_block):
        live = (lax.broadcasted_iota(jnp.int32, (row_block, PAIR), 0) + r0) < valid
        for i, cw_ref in enumerate((cwq_ref, cwk_ref, cwv_ref)):
            conv = None
            for t in range(CONV_W):
                start = pad - (CONV_W - 1) + t + r0
                term = xs_scr[i, start:start + row_block, :] * cw_ref[t:t + 1, :]
                conv = term if conv is None else conv + term
            act = _silu(conv)
            if i < 2:
                ss = _dot_exact_rhs(act * act, half_sum)
                act = act * lax.rsqrt(ss + EPS)
            act_scr[i, r0:r0 + row_block, :] = jnp.where(live, act, 0.0)
        gb = gb_ref[r0:r0 + row_block, :]
        for e in range(4):
            gate_scr[e, r0:r0 + row_block, :] = jnp.where(live, _dot_exact_rhs(gb, expand[e], terms=3), 0.0)

    ri = lax.broadcasted_iota(jnp.int32, (c2, c2), 0)
    ci = lax.broadcasted_iota(jnp.int32, (c2, c2), 1)
    delta = jnp.where((ri // c) == (ci // c), ri - ci, -1)
    causal = delta >= 0
    strict = delta > 0
    tril = jnp.where(causal, 1.0, 0.0).astype(BF16)
    eye = jnp.where(ri == ci, 1.0, 0.0)
    lane_c = lax.broadcasted_iota(jnp.int32, (c, PAIR), 1)
    first = lane_c < HEAD_DIM
    diag_blocks = (lax.broadcasted_iota(jnp.int32, (PAIR, PAIR), 0) // HEAD_DIM) == (
        lax.broadcasted_iota(jnp.int32, (PAIR, PAIR), 1) // HEAD_DIM)
    scale = HEAD_DIM ** -0.5

    def stack(x):
        return jnp.concatenate([jnp.where(first, x, 0.0), jnp.where(first, 0.0, x)], axis=0)

    def unstack(x):
        return x[:c] + x[c:]

    zeros = jnp.zeros((HEAD_DIM, HEAD_DIM), F32)
    s_init = jnp.concatenate([jnp.concatenate([s0_ref[0], zeros], axis=1),
                              jnp.concatenate([zeros, s0_ref[1]], axis=1)], axis=0)

    def prepare(ic):
        r0 = pl.multiple_of(ic * c, c)
        qc = act_scr[0, pl.ds(r0, c), :] * scale
        kc = act_scr[1, pl.ds(r0, c), :]
        vc = act_scr[2, pl.ds(r0, c), :]
        g_st = jnp.concatenate([gate_scr[0, pl.ds(r0, c), :], gate_scr[1, pl.ds(r0, c), :]], axis=0)
        b_st = jnp.concatenate([gate_scr[2, pl.ds(r0, c), :], gate_scr[3, pl.ds(r0, c), :]], axis=0)
        gh, gl = _split(g_st)
        gl2 = (g_st - gh.astype(F32) - gl.astype(F32)).astype(BF16)
        gc_st = _dot(tril, gh) + (_dot(tril, gl) + _dot(tril, gl2))
        diff = gc_st - gc_st.T
        decay = jnp.where(causal, jnp.exp(jnp.minimum(diff, 0.0)), 0.0)
        k_st = stack(kc)
        kb_st = k_st * b_st
        k_stb = k_st.astype(BF16)
        a_mat = jnp.where(strict, _dot(kb_st.astype(BF16), k_stb, NT) * decay, 0.0)
        qk = jnp.where(causal, _dot(stack(qc).astype(BF16), k_stb, NT) * decay, 0.0)
        base = jnp.where(ri // SUBLANES == ci // SUBLANES, a_mat, 0.0)
        b2 = _dot3(base, base)
        t_mat = eye - base
        t_mat = t_mat + _dot3(t_mat, b2)
        t_mat = t_mat + _dot3(t_mat, _dot3(b2, b2))
        size = SUBLANES
        while size < c:
            off = jnp.where(((ri // size) ^ (ci // size)) == 1, a_mat, 0.0)
            t_mat = t_mat - _dot3(_dot3(t_mat, off), t_mat)
            size *= 2
        rhs = jnp.concatenate([stack(vc) * b_st, kb_st * jnp.exp(gc_st)], axis=1)
        uw = _dot(t_mat.astype(BF16), rhs.astype(BF16))
        gc2 = jnp.where(first, gc_st[:c], gc_st[c:])
        g_last = gc2[c - 1:c, :]
        u_scr[ic] = unstack(uw[:, :PAIR])
        qgw_scr[ic] = jnp.concatenate([qc * jnp.exp(gc2), unstack(uw[:, PAIR:])], axis=0).astype(BF16)
        kdt_scr[ic] = (kc * jnp.exp(g_last - gc2)).T.astype(BF16)
        qk_scr[ic] = qk.astype(BF16)
        dec_scr[ic] = jnp.broadcast_to(jnp.exp(g_last), (SUBLANES, PAIR))

    n_chunks = seq // c
    if n_chunks % 2 == 0:
        def prepare_two(i, carry):
            prepare(2 * i)
            prepare(2 * i + 1)
            return carry
        lax.fori_loop(0, n_chunks // 2, prepare_two, 0)
    else:
        def prepare_one(i, carry):
            prepare(i)
            return carry
        lax.fori_loop(0, n_chunks, prepare_one, 0)

    def recur(ic, s2):
        r0 = pl.multiple_of(ic * c, c)
        xs = _dot(qgw_scr[ic], s2.astype(BF16))
        v_new = u_scr[ic] - xs[c:]
        o2 = xs[:c] + unstack(_dot(qk_scr[ic], stack(v_new).astype(BF16)))
        upd = _dot(kdt_scr[ic], v_new.astype(BF16))
        s2 = s2 * dec_scr[ic, 0:1, :] + jnp.where(diag_blocks, upd, 0.0)
        ms = _dot_exact_rhs(o2 * o2, half_mean)
        zc = z_ref[pl.ds(r0, c), :]
        o_ref[pl.ds(r0, c), :] = o2 * lax.rsqrt(ms + EPS) * ng_ref[...] * _silu(zc)
        return s2

    s_fin = lax.fori_loop(0, n_chunks, recur, s_init)
    s_out_ref[0] = s_fin[:HEAD_DIM, :HEAD_DIM]
    s_out_ref[1] = s_fin[HEAD_DIM:, HEAD_DIM:]


def _gdn(gq, z, gb, conv_w, conv_buf, s0, ng, n_seq, seq, valid):
    row_block = min(seq, 256)
    n_chunks = seq // GDN_CHUNK
    seq_map = lambda off: (lambda b, hp: (b, 0, off + hp))
    cw_map = lambda off: (lambda b, hp: (0, off + hp))
    kern = functools.partial(_gdn_kernel, seq=seq, valid=valid, row_block=row_block)
    sb = pl.BlockSpec((None, seq, PAIR), seq_map(0))
    return pl.pallas_call(
        kern,
        grid=(n_seq, N_PAIRS),
        in_specs=[pl.BlockSpec((None, seq, PAIR), seq_map(0)), pl.BlockSpec((None, seq, PAIR), seq_map(N_PAIRS)),
                  pl.BlockSpec((None, seq, PAIR), seq_map(2 * N_PAIRS)), sb,
                  pl.BlockSpec((None, seq, LANES), lambda b, hp: (b, 0, 0)),
                  pl.BlockSpec((CONV_W, PAIR), cw_map(0)), pl.BlockSpec((CONV_W, PAIR), cw_map(N_PAIRS)),
                  pl.BlockSpec((CONV_W, PAIR), cw_map(2 * N_PAIRS)),
                  pl.BlockSpec((None, CONV_W - 1, PAIR), seq_map(0)),
                  pl.BlockSpec((None, CONV_W - 1, PAIR), seq_map(N_PAIRS)),
                  pl.BlockSpec((None, CONV_W - 1, PAIR), seq_map(2 * N_PAIRS)),
                  pl.BlockSpec((None, 2, HEAD_DIM, HEAD_DIM), lambda b, hp: (b, hp, 0, 0)),
                  pl.BlockSpec((1, PAIR), lambda b, hp: (0, 0))],
        out_specs=[sb, pl.BlockSpec((None, 2, HEAD_DIM, HEAD_DIM), lambda b, hp: (b, hp, 0, 0))],
        out_shape=[jax.ShapeDtypeStruct((n_seq, seq, WIDTH), F32),
                   jax.ShapeDtypeStruct((n_seq, N_HEADS, HEAD_DIM, HEAD_DIM), F32)],
        scratch_shapes=[pltpu.VMEM((3, seq + SUBLANES, PAIR), F32), pltpu.VMEM((3, seq, PAIR), F32),
                        pltpu.VMEM((4, seq, PAIR), F32),
                        pltpu.VMEM((n_chunks, GDN_CHUNK, PAIR), F32),
                        pltpu.VMEM((n_chunks, 2 * GDN_CHUNK, PAIR), BF16),
                        pltpu.VMEM((n_chunks, PAIR, GDN_CHUNK), BF16),
                        pltpu.VMEM((n_chunks, 2 * GDN_CHUNK, 2 * GDN_CHUNK), BF16),
                        pltpu.VMEM((n_chunks, SUBLANES, PAIR), F32)],
        compiler_params=pltpu.CompilerParams(dimension_semantics=("arbitrary",) * 2, vmem_limit_bytes=VMEM_LIMIT),
        name="gdn",
    )(gq, gq, gq, z, gb, conv_w, conv_w, conv_w, conv_buf, conv_buf, conv_buf, s0, ng)


def _outproj_kernel(x_ref, att_ref, gdn_ref, wo_ref, g2_ref, wr_ref, br_ref, h_ref, m_ref, ti_ref, tg_ref):
    mix = jnp.concatenate([att_ref[...], gdn_ref[...]], axis=1).astype(BF16)
    h = x_ref[...] + _dot(mix, wo_ref[...])
    h_ref[...] = h
    ms = jnp.mean(h * h, axis=-1, keepdims=True)
    m = h * lax.rsqrt(ms + EPS) * g2_ref[...]
    for s in range(D_MODEL // LANES):
        m_ref[:, s, :] = m[:, s * LANES:(s + 1) * LANES]
    logits = _dot3(m, wr_ref[...]) + br_ref[...]
    lane = lax.broadcasted_iota(jnp.int32, logits.shape, 1)
    ti = jnp.zeros(logits.shape, jnp.int32)
    tv = jnp.full(logits.shape, NEG, F32)
    for r in range(TOP_K):
        mx = jnp.max(logits, axis=1, keepdims=True)
        idx = jnp.min(jnp.where(logits == mx, lane, LANES), axis=1, keepdims=True)
        ti = jnp.where(lane == r, idx, ti)
        tv = jnp.where(lane == r, mx, tv)
        logits = jnp.where(lane == idx, -jnp.inf, logits)
    e = jnp.exp(tv - jnp.max(tv, axis=1, keepdims=True))
    ti_ref[...] = ti
    tg_ref[...] = e / jnp.sum(e, axis=1, keepdims=True)


def _outproj(x2, att, gdn, wo, g2, wr_pad, br_pad, tm):
    t = x2.shape[0]
    row = lambda i: (i, 0)
    fixed = lambda i: (0, 0)
    return pl.pallas_call(
        _outproj_kernel,
        grid=(t // tm,),
        in_specs=[pl.BlockSpec((tm, D_MODEL), row), pl.BlockSpec((tm, WIDTH), row), pl.BlockSpec((tm, WIDTH), row),
                  pl.BlockSpec((D_MODEL, D_MODEL), fixed), pl.BlockSpec((1, D_MODEL), fixed),
                  pl.BlockSpec((D_MODEL, LANES), fixed), pl.BlockSpec((1, LANES), fixed)],
        out_specs=[pl.BlockSpec((tm, D_MODEL), row),
                   pl.BlockSpec((tm, D_MODEL // LANES, LANES), lambda i: (i, 0, 0)),
                   pl.BlockSpec((tm, LANES), row), pl.BlockSpec((tm, LANES), row)],
        out_shape=[jax.ShapeDtypeStruct((t, D_MODEL), F32),
                   jax.ShapeDtypeStruct((t, D_MODEL // LANES, LANES), F32),
                   jax.ShapeDtypeStruct((t, LANES), jnp.int32), jax.ShapeDtypeStruct((t, LANES), F32)],
        compiler_params=pltpu.CompilerParams(dimension_semantics=("arbitrary",), vmem_limit_bytes=VMEM_LIMIT),
        name="outproj",
    )(x2, att, gdn, wo, g2, wr_pad, br_pad)


def _dispatch_kernel(dest_ref, m_ref, xin_ref, xbuf_ref, sem, *, tb):
    del xin_ref

    def row_copy(t, k):
        return pltpu.make_async_copy(m_ref.at[t], xbuf_ref.at[dest_ref[t * TOP_K + k]], sem)

    def issue(t, _):
        for k in range(TOP_K):
            row_copy(t, k).start()
        return 0

    lax.fori_loop(0, tb, issue, 0)

    def drain(t, _):
        for k in range(TOP_K):
            row_copy(t, k).wait()
        return 0

    lax.fori_loop(0, tb, drain, 0)


def _dispatch(dest_flat, m3, xbuf, tb):
    t = m3.shape[0]
    return pl.pallas_call(
        functools.partial(_dispatch_kernel, tb=tb),
        grid=(t // tb,),
        in_specs=[pl.BlockSpec((tb * TOP_K,), lambda i: (i,), memory_space=pltpu.SMEM),
                  pl.BlockSpec((tb, D_MODEL // LANES, LANES), lambda i: (i, 0, 0)),
                  pl.BlockSpec(memory_space=pl.ANY)],
        out_specs=pl.BlockSpec(memory_space=pl.ANY),
        out_shape=jax.ShapeDtypeStruct(xbuf.shape, xbuf.dtype),
        scratch_shapes=[pltpu.SemaphoreType.DMA(())],
        input_output_aliases={2: 0},
        compiler_params=pltpu.CompilerParams(dimension_semantics=("arbitrary",)),
        name="moe_dispatch",
    )(dest_flat, m3, xbuf)


def _split_gate_up_kernel(w_ref, wg_ref, wl_ref):
    n = 2 * LANES
    r = lax.broadcasted_iota(jnp.int32, (n, n), 0)
    c = lax.broadcasted_iota(jnp.int32, (n, n), 1)
    perm = jnp.where(r == jnp.where(c < LANES, 2 * c, 2 * (c - LANES) + 1), 1.0, 0.0).astype(BF16)
    for g in range(w_ref.shape[1] // n):
        y = _dot(w_ref[:, g * n:(g + 1) * n].astype(BF16), perm)
        wg_ref[:, g * LANES:(g + 1) * LANES] = y[:, :LANES].astype(BF16)
        wl_ref[:, g * LANES:(g + 1) * LANES] = y[:, LANES:].astype(BF16)


def _split_gate_up(w_gate_up):
    n_e, d_in, d_gu = w_gate_up.shape
    cols = 4 * LANES
    out = jax.ShapeDtypeStruct((n_e, d_in, d_gu // 2), BF16)
    return pl.pallas_call(
        _split_gate_up_kernel,
        grid=(n_e, d_gu // cols),
        in_specs=[pl.BlockSpec((None, d_in, cols), lambda e, j: (e, 0, j))],
        out_specs=[pl.BlockSpec((None, d_in, cols // 2), lambda e, j: (e, 0, j))] * 2,
        out_shape=[out, out],
        compiler_params=pltpu.CompilerParams(dimension_semantics=("arbitrary",) * 2, vmem_limit_bytes=VMEM_LIMIT),
        name="moe_split_gate_up",
    )(w_gate_up)


def _experts_kernel(be_ref, nb_ref, x_ref, wg_ref, wl_ref, wd_ref, bg_ref, bl_ref, bd_ref, y_ref):
    i = pl.program_id(0)
    n_s = D_MODEL // LANES

    @pl.when(i < nb_ref[0])
    def _():
        x = jnp.concatenate([x_ref[:, s, :] for s in range(n_s)], axis=1).astype(BF16)
        gate = jnp.minimum(_dot(x, wg_ref[...]) + bg_ref[...], SWIGLU_LIMIT)
        lin = jnp.clip(_dot(x, wl_ref[...]) + bl_ref[...], -SWIGLU_LIMIT, SWIGLU_LIMIT)
        hmid = gate * _sigmoid(SWIGLU_ALPHA * gate) * (lin + 1.0)
        y = _dot(hmid.astype(BF16), wd_ref[...]) + bd_ref[...]
        for s in range(n_s):
            y_ref[:, s, :] = y[:, s * LANES:(s + 1) * LANES]

    @pl.when(i >= nb_ref[0])
    def _():
        y_ref[...] = jnp.zeros(y_ref.shape, y_ref.dtype)


def _experts(block_e, nb_used, xbuf, wg, wl, wd, bg, bl, bd):
    n_rows = xbuf.shape[0]
    n_s = D_MODEL // LANES
    d_ff = wg.shape[2]
    xmap = lambda i, be, nb: (jnp.minimum(i, nb[0] - 1), 0, 0)
    wmap = lambda i, be, nb: (be[i], 0, 0)
    return pl.pallas_call(
        _experts_kernel,
        grid_spec=pltpu.PrefetchScalarGridSpec(
            num_scalar_prefetch=2,
            grid=(n_rows // EXPERT_ROWS,),
            in_specs=[pl.BlockSpec((EXPERT_ROWS, n_s, LANES), xmap),
                      pl.BlockSpec((None, D_MODEL, d_ff), wmap), pl.BlockSpec((None, D_MODEL, d_ff), wmap),
                      pl.BlockSpec((None, d_ff, D_MODEL), wmap),
                      pl.BlockSpec((None, 1, d_ff), wmap), pl.BlockSpec((None, 1, d_ff), wmap),
                      pl.BlockSpec((None, 1, D_MODEL), wmap)],
            out_specs=pl.BlockSpec((EXPERT_ROWS, n_s, LANES), lambda i, be, nb: (i, 0, 0))),
        out_shape=jax.ShapeDtypeStruct(xbuf.shape, F32),
        compiler_params=pltpu.CompilerParams(dimension_semantics=("arbitrary",), vmem_limit_bytes=VMEM_LIMIT),
        name="moe_experts",
    )(block_e, nb_used, xbuf, wg, wl, wd, bg, bl, bd)


def _combine_kernel(dest_ref, h_ref, tg_ref, y_ref, o_ref, buf, sem, *, tb):
    n_s = D_MODEL // LANES

    def row_copy(t, k):
        return pltpu.make_async_copy(y_ref.at[dest_ref[t * TOP_K + k]], buf.at[k, t], sem)

    def issue(t, _):
        for k in range(TOP_K):
            row_copy(t, k).start()
        return 0

    lax.fori_loop(0, tb, issue, 0)

    def drain(t, _):
        for k in range(TOP_K):
            row_copy(t, k).wait()
        return 0

    lax.fori_loop(0, tb, drain, 0)
    tg = tg_ref[...]
    for s in range(n_s):
        acc = h_ref[:, s * LANES:(s + 1) * LANES]
        for k in range(TOP_K):
            acc = acc + tg[:, k:k + 1] * buf[k, :, s, :]
        o_ref[:, s * LANES:(s + 1) * LANES] = acc


def _combine(dest_flat, h, tg, ybuf, tb):
    t = h.shape[0]
    n_s = D_MODEL // LANES
    row = lambda i: (i, 0)
    return pl.pallas_call(
        functools.partial(_combine_kernel, tb=tb),
        grid=(t // tb,),
        in_specs=[pl.BlockSpec((tb * TOP_K,), lambda i: (i,), memory_space=pltpu.SMEM),
                  pl.BlockSpec((tb, D_MODEL), row), pl.BlockSpec((tb, LANES), row),
                  pl.BlockSpec(memory_space=pl.ANY)],
        out_specs=pl.BlockSpec((tb, D_MODEL), row),
        out_shape=jax.ShapeDtypeStruct((t, D_MODEL), F32),
        scratch_shapes=[pltpu.VMEM((TOP_K, tb, n_s, LANES), F32), pltpu.SemaphoreType.DMA(())],
        compiler_params=pltpu.CompilerParams(dimension_semantics=("arbitrary",), vmem_limit_bytes=VMEM_LIMIT),
        name="moe_combine",
    )(dest_flat, h, tg, ybuf)


def _routing(flat_e, n_blocks):
    onehot = (flat_e[:, None] == jnp.arange(N_EXPERTS, dtype=jnp.int32)[None, :]).astype(jnp.int32)
    csum = jnp.cumsum(onehot, axis=0)
    rank = jnp.sum(onehot * csum, axis=1) - 1
    counts = csum[-1]
    padded = (counts + EXPERT_ROWS - 1) // EXPERT_ROWS * EXPERT_ROWS
    pad_end = jnp.cumsum(padded)
    pad_start = pad_end - padded
    dest = (pad_start[flat_e] + rank).astype(jnp.int32)
    block_start = jnp.arange(n_blocks, dtype=jnp.int32) * EXPERT_ROWS
    block_e = jnp.minimum(jnp.sum((pad_end[None, :] <= block_start[:, None]).astype(jnp.int32), axis=1),
                          N_EXPERTS - 1).astype(jnp.int32)
    nb_used = (pad_end[-1:] // EXPERT_ROWS).astype(jnp.int32)
    return dest, block_e, nb_used


def kernel(x_prompt, x_sample, cache_k, cache_v, page_table, state_gdn, state_conv, norm1_g, w_in, q_norm_g,
           k_norm_g, conv_w, a_log, dt_bias, gdn_norm_g, w_out, norm2_g, w_router, b_router, w_gate_up,
           b_gate_up, w_down, b_down):
    assert w_in.shape[0] == 1, "single layer"
    b, s, d = x_prompt.shape
    db, dl, _ = x_sample.shape
    tp, ts = b * s, db * dl

    w_in_p = jnp.pad(w_in[0], ((0, 0), (0, IN_COLS_PAD - IN_COLS))).astype(BF16)
    qg = jnp.tile(q_norm_g[0], N_HEADS)[None, :]
    kg = jnp.tile(k_norm_g[0], N_HEADS)[None, :]
    neg_a = jnp.pad(-jnp.exp(a_log[0]), (0, LANES - N_HEADS))[None, :]
    dtb = jnp.pad(dt_bias[0], (0, LANES - N_HEADS))[None, :]
    ng = jnp.tile(gdn_norm_g[0], 2)[None, :]
    wo = w_out[0].astype(BF16)
    wr = jnp.pad(w_router[0], ((0, 0), (0, LANES - N_EXPERTS)))
    br = jnp.pad(b_router[0], (0, LANES - N_EXPERTS), constant_values=NEG)[None, :]
    wg, wl = _split_gate_up(w_gate_up[0])
    bg = b_gate_up[0][:, None, 0::2]
    bl = b_gate_up[0][:, None, 1::2]
    wd = w_down[0].astype(BF16)
    bd = b_down[0][:, None, :]
    slopes = (2.0 ** (-8.0 * jnp.arange(1, N_HEADS + 1, dtype=F32) / N_HEADS)).astype(F32)

    xp2 = x_prompt.reshape(tp, d)
    xs2 = x_sample.reshape(ts, d)
    g1 = norm1_g[0][None, :]
    g2 = norm2_g[0][None, :]

    qp, kp, vp, gqp, zp, gbp = _inproj(xp2, g1, w_in_p, qg, kg, neg_a, dtb, tm=512)
    att_p = _moba_prompt(slopes, qp, kp, vp, b, s)
    gdn_p, gstate_p = _gdn(gqp.reshape(b, s, GDN_QKV), zp.reshape(b, s, WIDTH), gbp.reshape(b, s, LANES),
                           conv_w[0], jnp.zeros((b, CONV_W - 1, GDN_QKV), F32),
                           jnp.zeros((b, N_HEADS, HEAD_DIM, HEAD_DIM), F32), ng, b, s, s)

    qs, ks, vs, gqs, zs, gbs = _inproj(xs2, g1, w_in_p, qg, kg, neg_a, dtb, tm=ts)
    page_major = lambda c: jnp.transpose(c[0], (0, 2, 3, 1)).reshape(c.shape[1], WIDTH, PAGE)
    att_s = _moba_sample(page_table, qs, ks, vs, page_major(cache_k), page_major(cache_v), db, dl)
    padl = lambda a: jnp.pad(a.reshape(db, dl, -1), ((0, 0), (0, GDN_CHUNK - dl), (0, 0)))
    gdn_s, gstate_s = _gdn(padl(gqs), padl(zs), padl(gbs), conv_w[0], state_conv[0], state_gdn[0], ng,
                           db, GDN_CHUNK, dl)
    gdn_s = gdn_s[:, :dl].reshape(ts, WIDTH)

    hp, mp, tip, tgp = _outproj(xp2, att_p, gdn_p.reshape(tp, WIDTH), wo, g2, wr, br, tm=512)
    hs, msm, tis, tgs = _outproj(xs2, att_s, gdn_s, wo, g2, wr, br, tm=ts)

    n_assign = (tp + ts) * TOP_K
    n_blocks = (n_assign + N_EXPERTS * (EXPERT_ROWS - 1)) // EXPERT_ROWS + 1
    flat_e = jnp.concatenate([tip[:, :TOP_K].reshape(-1), tis[:, :TOP_K].reshape(-1)])
    dest, block_e, nb_used = _routing(flat_e, n_blocks)
    dest_p, dest_s = dest[:tp * TOP_K], dest[tp * TOP_K:]
    xbuf = jnp.zeros((n_blocks * EXPERT_ROWS, D_MODEL // LANES, LANES), F32)
    xbuf = _dispatch(dest_p, mp, xbuf, tb=256)
    xbuf = _dispatch(dest_s, msm, xbuf, tb=ts)
    ybuf = _experts(block_e, nb_used, xbuf, wg, wl, wd, bg, bl, bd)
    y_prompt = _combine(dest_p, hp, tgp, ybuf, tb=256).reshape(b, s, d)
    y_sample = _combine(dest_s, hs, tgs, ybuf, tb=ts).reshape(db, dl, d)

    k_prompt = kp.reshape(1, b, s // PAGE, PAGE, N_HEADS, HEAD_DIM)
    v_prompt = vp.reshape(1, b, s // PAGE, PAGE, N_HEADS, HEAD_DIM)
    k_sample = ks.reshape(1, db, dl, N_HEADS, HEAD_DIM)
    v_sample = vs.reshape(1, db, dl, N_HEADS, HEAD_DIM)
    conv_prompt = gqp.reshape(b, s, GDN_QKV)[:, s - (CONV_W - 1):][None]
    conv_sample = jnp.concatenate([state_conv[0], gqs.reshape(db, dl, GDN_QKV)], axis=1)[:, -(CONV_W - 1):][None]
    return (y_prompt, y_sample, k_prompt, v_prompt, k_sample, v_sample,
            gstate_p[None], conv_prompt, gstate_s[None], conv_sample)
```

```python
import functools

import jax
import jax.numpy as jnp
import numpy as np
from jax import lax
from jax.experimental import pallas as pl
from jax.experimental.pallas import tpu as pltpu

F32 = jnp.float32
BF16 = jnp.bfloat16

LANES = 128
SUBLANES = 8
HEAD_DIM = 64
N_HEADS = 8
PAIR = 2 * HEAD_DIM
N_PAIRS = N_HEADS // 2
WIDTH = N_HEADS * HEAD_DIM
D_MODEL = 1024
GDN_QKV = 3 * WIDTH
IN_COLS = 3 * WIDTH + GDN_QKV + WIDTH + 2 * N_HEADS
IN_COLS_PAD = 3712
OFF_GQ = 3 * WIDTH
OFF_Z = OFF_GQ + GDN_QKV
OFF_AB = OFF_Z + WIDTH
MOBA_BLOCK = 256
MOBA_TOPK = 3
Q_CHUNK = 128
PAGE = 128
CONV_W = 4
GDN_CHUNK = 64
N_EXPERTS = 32
TOP_K = 4
SWIGLU_LIMIT = 7.0
SWIGLU_ALPHA = 1.702
EXPERT_ROWS = 256
SAMPLE_PAGES_PER_STEP = 16
SAMPLE_UNROLL = 4
GDN_GROUP = 4
EPS = 1e-6
NEG = -1e30
VMEM_LIMIT = 56 * 1024 * 1024

NN = (((1,), (0,)), ((), ()))
NT = (((1,), (1,)), ((), ()))


def _dot(a, b, dims=NN):
    return lax.dot_general(a, b, dims, preferred_element_type=F32)


def _split(x):
    hi = x.astype(BF16)
    lo = (x - hi.astype(F32)).astype(BF16)
    return hi, lo


def _dot3(a, b, dims=NN):
    ah, al = _split(a)
    bh, bl = _split(b)
    return _dot(ah, bh, dims) + (_dot(ah, bl, dims) + _dot(al, bh, dims))


def _dot_exact_rhs(a, b_bf16, terms=2):
    out = None
    r = a
    for _ in range(terms):
        h = r.astype(BF16)
        part = _dot(h, b_bf16)
        out = part if out is None else out + part
        r = r - h.astype(F32)
    return out


def _dot_exact_lhs(a_bf16, b, terms=2):
    out = None
    r = b
    for _ in range(terms):
        h = r.astype(BF16)
        part = _dot(a_bf16, h)
        out = part if out is None else out + part
        r = r - h.astype(F32)
    return out


def _group_matrix(n, group, value):
    r = lax.broadcasted_iota(jnp.int32, (n, n), 0) // group
    c = lax.broadcasted_iota(jnp.int32, (n, n), 1) // group
    return jnp.where(r == c, value, 0.0).astype(BF16)


def _group_mean_matrix(n, group):
    return _group_matrix(n, group, 1.0 / group)


def _sigmoid(x):
    return 1.0 / (1.0 + jnp.exp(-x))


def _silu(x):
    return x * _sigmoid(x)


def _inproj_kernel(x_ref, g1_ref, w_ref, qg_ref, kg_ref, neg_a_ref, dtb_ref,
                   q_ref, k_ref, v_ref, gq_ref, z_ref, gb_ref):
    x = x_ref[...]
    ms = jnp.mean(x * x, axis=-1, keepdims=True)
    n = x * lax.rsqrt(ms + EPS) * g1_ref[...]
    p = _dot(n.astype(BF16), w_ref[...])
    gm = _group_mean_matrix(2 * LANES, HEAD_DIM)

    def head_norm(t, gain):
        sq = t * t
        parts = [_dot_exact_rhs(sq[:, c:c + 2 * LANES], gm) for c in range(0, WIDTH, 2 * LANES)]
        return t * lax.rsqrt(jnp.concatenate(parts, axis=1) + EPS) * gain

    q_ref[...] = head_norm(p[:, 0:WIDTH], qg_ref[...])
    k_ref[...] = head_norm(p[:, WIDTH:2 * WIDTH], kg_ref[...])
    v_ref[...] = p[:, 2 * WIDTH:3 * WIDTH]
    gq_ref[...] = p[:, OFF_GQ:OFF_Z]
    z_ref[...] = p[:, OFF_Z:OFF_AB]
    ab = p[:, OFF_AB:IN_COLS_PAD]
    t = ab + dtb_ref[...]
    softplus = jnp.maximum(t, 0.0) + jnp.log(1.0 + jnp.exp(-jnp.abs(t)))
    lane = lax.broadcasted_iota(jnp.int32, ab.shape, 1)
    gb_ref[...] = jnp.where(lane < N_HEADS, neg_a_ref[...] * softplus, _sigmoid(ab))


def _inproj(x2, g1, w_pad, qg, kg, neg_a, dtb, tm):
    t = x2.shape[0]
    row = lambda i: (i, 0)
    fixed = lambda i: (0, 0)
    outs = [jax.ShapeDtypeStruct((t, WIDTH), F32)] * 3 + [
        jax.ShapeDtypeStruct((t, GDN_QKV), F32), jax.ShapeDtypeStruct((t, WIDTH), F32),
        jax.ShapeDtypeStruct((t, LANES), F32)]
    return pl.pallas_call(
        _inproj_kernel,
        grid=(t // tm,),
        in_specs=[pl.BlockSpec((tm, D_MODEL), row), pl.BlockSpec((1, D_MODEL), fixed),
                  pl.BlockSpec((D_MODEL, IN_COLS_PAD), fixed), pl.BlockSpec((1, WIDTH), fixed),
                  pl.BlockSpec((1, WIDTH), fixed), pl.BlockSpec((1, LANES), fixed),
                  pl.BlockSpec((1, LANES), fixed)],
        out_specs=[pl.BlockSpec((tm, WIDTH), row)] * 3 + [
            pl.BlockSpec((tm, GDN_QKV), row), pl.BlockSpec((tm, WIDTH), row), pl.BlockSpec((tm, LANES), row)],
        out_shape=outs,
        compiler_params=pltpu.CompilerParams(dimension_semantics=("arbitrary",), vmem_limit_bytes=VMEM_LIMIT),
        name="inproj",
    )(x2, g1, w_pad, qg, kg, neg_a, dtb)


def _moba_prompt_kernel(slopes_ref, q_ref, k_ref, v_ref, o_ref, kmean_ref, vt_ref, acc_ref, *, n_blocks):
    hp = pl.program_id(1)
    ob = pl.program_id(2)

    @pl.when(ob == 0)
    def _():
        for n in range(n_blocks):
            blk = slice(n * MOBA_BLOCK, (n + 1) * MOBA_BLOCK)
            kmean_ref[n:n + 1, :] = jnp.mean(k_ref[blk, :], axis=0, keepdims=True)
            vt_ref[:, blk] = v_ref[blk, :].T.astype(BF16)

    bq = MOBA_BLOCK
    cols = 2 * bq
    q = q_ref[...]
    lane = lax.broadcasted_iota(jnp.int32, (bq, PAIR), 1)
    qst = jnp.concatenate([jnp.where(lane < HEAD_DIM, q, 0.0), jnp.where(lane < HEAD_DIM, 0.0, q)], axis=0)
    qst_b = (qst * HEAD_DIM ** -0.5).astype(BF16)
    bidx = lax.broadcasted_iota(jnp.int32, (n_blocks, cols), 0)
    gate = jnp.where(bidx < ob, _dot3(kmean_ref[...], qst, NT), -jnp.inf)
    valid = jnp.where(bidx < ob, 1.0, 0.0)
    sel = jnp.zeros((n_blocks, cols), F32)
    for _ in range(min(MOBA_TOPK, n_blocks - 1)):
        mx = jnp.max(gate, axis=0, keepdims=True)
        idx = jnp.min(jnp.where(gate == mx, bidx, n_blocks), axis=0, keepdims=True)
        hit = bidx == idx
        sel = jnp.where(hit, valid, sel)
        gate = jnp.where(hit, -jnp.inf, gate)

    col = lax.broadcasted_iota(jnp.int32, (1, cols), 1)
    slope = jnp.where(col < bq, slopes_ref[2 * hp], slopes_ref[2 * hp + 1])
    rel = (lax.broadcasted_iota(jnp.int32, (bq, cols), 1) % bq
           - lax.broadcasted_iota(jnp.int32, (bq, cols), 0)).astype(F32)
    alibi = slope * rel

    def logits(n):
        start = pl.multiple_of(n * MOBA_BLOCK, MOBA_BLOCK)
        kb = k_ref[pl.ds(start, MOBA_BLOCK), :].astype(BF16)
        return _dot(kb, qst_b, NT) - alibi, start

    s, start = logits(ob)
    s = jnp.where(rel >= 0, s, NEG)
    m0 = jnp.max(s, axis=0, keepdims=True)
    p = jnp.exp(s - m0)
    l0 = jnp.sum(p, axis=0, keepdims=True)
    acc_ref[...] = _dot(vt_ref[:, pl.ds(start, MOBA_BLOCK)], p.astype(BF16))

    def body(n, carry):
        m_i, l_i = carry
        s, start = logits(n)
        sel_n = jnp.sum(jnp.where(bidx == n, sel, 0.0), axis=0, keepdims=True)
        far = slope * jnp.full((1, cols), (ob - n) * MOBA_BLOCK, jnp.int32).astype(F32)
        s = jnp.where(sel_n > 0, s - far, NEG)
        m_new = jnp.maximum(m_i, jnp.max(s, axis=0, keepdims=True))
        alpha = jnp.exp(m_i - m_new)
        p = jnp.exp(s - m_new)
        acc_ref[...] = acc_ref[...] * alpha + _dot(vt_ref[:, pl.ds(start, MOBA_BLOCK)], p.astype(BF16))
        return m_new, l_i * alpha + jnp.sum(p, axis=0, keepdims=True)

    _, l_f = lax.fori_loop(0, ob, body, (m0, l0))
    o_t = acc_ref[...] * (1.0 / l_f)
    o_ref[...] = jnp.concatenate([o_t[:HEAD_DIM, :bq], o_t[HEAD_DIM:, bq:]], axis=0).T


def _moba_prompt(slopes, q, k, v, batch, seq):
    n_blocks = seq // MOBA_BLOCK
    qmap = lambda b, hp, ob: (b * n_blocks + ob, hp)
    kmap = lambda b, hp, ob: (b, hp)
    return pl.pallas_call(
        functools.partial(_moba_prompt_kernel, n_blocks=n_blocks),
        grid=(batch, N_PAIRS, n_blocks),
        in_specs=[pl.BlockSpec(memory_space=pltpu.SMEM),
                  pl.BlockSpec((MOBA_BLOCK, PAIR), qmap), pl.BlockSpec((seq, PAIR), kmap),
                  pl.BlockSpec((seq, PAIR), kmap)],
        out_specs=pl.BlockSpec((MOBA_BLOCK, PAIR), qmap),
        out_shape=jax.ShapeDtypeStruct((batch * seq, WIDTH), F32),
        scratch_shapes=[pltpu.VMEM((n_blocks, PAIR), F32), pltpu.VMEM((PAIR, seq), BF16),
                        pltpu.VMEM((PAIR, 2 * MOBA_BLOCK), F32)],
        compiler_params=pltpu.CompilerParams(dimension_semantics=("arbitrary",) * 3, vmem_limit_bytes=VMEM_LIMIT),
        name="moba_prompt",
    )(slopes, q, k, v)


def _moba_sample_kernel(pt_ref, q_ref, kn_ref, vn_ref, *refs, n_blk, n_new, past, pps):
    k_refs, v_refs = refs[:pps], refs[pps:2 * pps]
    o_ref, q_scr, s_scr, l_scr, acc_scr, gate_scr = refs[2 * pps:]
    ph = pl.program_id(1)
    j = pl.program_id(2)
    rows = N_HEADS * n_new
    scale = HEAD_DIM ** -0.5
    n_steps = (past // PAGE) // pps
    head_of_lane = lax.broadcasted_iota(jnp.int32, (n_new, WIDTH), 1) // HEAD_DIM

    @pl.when((ph == 0) & (j == 0))
    def _():
        q = q_ref[...]
        for h in range(N_HEADS):
            q_scr[h * n_new:(h + 1) * n_new, :] = jnp.where(head_of_lane == h, q, 0.0)

        gate_scr[...] = jnp.full((rows, LANES), -jnp.inf, F32)

    ppb = MOBA_BLOCK // PAGE
    lane = lax.broadcasted_iota(jnp.int32, (rows, LANES), 1)

    @pl.when(ph == 0)
    def _():
        qb = q_scr[...].astype(BF16)
        gate = gate_scr[...]
        for blk in range(pps // ppb):
            total = None
            for pg in range(blk * ppb, (blk + 1) * ppb):
                col = pl.multiple_of((j * pps + pg) * PAGE, PAGE)
                s = _dot(qb, k_refs[pg][...].astype(BF16))
                s_scr[:, pl.ds(col, PAGE)] = s
                total = s if total is None else total + s
            mean = jnp.sum(total, axis=1, keepdims=True) * (1.0 / MOBA_BLOCK)
            gate = jnp.where(lane == j * (pps // ppb) + blk, mean, gate)
        gate_scr[...] = gate

    slope_col = jnp.concatenate(
        [jnp.full((n_new, 1), 2.0 ** (-8.0 * (h + 1) / N_HEADS), F32) for h in range(N_HEADS)], axis=0)
    qpos = (past + lax.broadcasted_iota(jnp.int32, (rows, 1), 0) % n_new).astype(F32)

    def block_logits(n):
        start = pl.multiple_of(n * MOBA_BLOCK, MOBA_BLOCK)
        raw = s_scr[:, pl.ds(start, MOBA_BLOCK)]
        kpos = (n * MOBA_BLOCK + lax.broadcasted_iota(jnp.int32, (rows, MOBA_BLOCK), 1)).astype(F32)
        return raw * scale - slope_col * (qpos - kpos), start

    @pl.when((ph == 1) & (j == 0))
    def _():
        gate = gate_scr[...]
        sel = jnp.zeros((rows, LANES), F32)
        for _ in range(min(MOBA_TOPK, n_blk)):
            mx = jnp.max(gate, axis=1, keepdims=True)
            idx = jnp.min(jnp.where(gate == mx, lane, LANES), axis=1, keepdims=True)
            hit = lane == idx
            sel = jnp.where(hit, 1.0, sel)
            gate = jnp.where(hit, -jnp.inf, gate)

        own = _dot(q_scr[...], kn_ref[...], NT)
        li = lax.broadcasted_iota(jnp.int32, (rows, n_new), 0) % n_new
        ri = lax.broadcasted_iota(jnp.int32, (rows, n_new), 1)
        d_own = (li - ri).astype(F32)
        lg_own = jnp.where(d_own >= 0, own * scale - slope_col * d_own, NEG)
        m0 = jnp.max(lg_own, axis=1, keepdims=True)

        def masked_logits(n):
            lg, start = block_logits(n)
            picked = jnp.sum(jnp.where(lane == n, sel, 0.0), axis=1, keepdims=True) > 0
            return jnp.where(picked, lg, NEG), start

        def max_body(n, mv):
            lg, _ = masked_logits(n)
            return jnp.maximum(mv, jnp.maximum(lg[:, :LANES], lg[:, LANES:]))

        mv = lax.fori_loop(0, n_blk, max_body, jnp.full((rows, LANES), NEG, F32), unroll=SAMPLE_UNROLL)
        m = jnp.maximum(m0, jnp.max(mv, axis=1, keepdims=True))
        p_own = jnp.exp(lg_own - m)

        def p_body(n, lv):
            lg, start = masked_logits(n)
            p = jnp.exp(lg - m)
            s_scr[:, pl.ds(start, MOBA_BLOCK)] = p
            return lv + (p[:, :LANES] + p[:, LANES:])

        lv = lax.fori_loop(0, n_blk, p_body, jnp.zeros((rows, LANES), F32), unroll=SAMPLE_UNROLL)
        l = jnp.sum(p_own, axis=1, keepdims=True) + jnp.sum(lv, axis=1, keepdims=True)
        l_scr[...] = jnp.broadcast_to(l, (rows, LANES))
        acc_scr[...] = _dot(p_own, vn_ref[...])

    @pl.when(ph == 1)
    def _():
        acc = acc_scr[...]
        for pg in range(pps):
            col = pl.multiple_of((j * pps + pg) * PAGE, PAGE)
            acc = acc + _dot(s_scr[:, pl.ds(col, PAGE)].astype(BF16), v_refs[pg][...].astype(BF16), NT)
        acc_scr[...] = acc

    @pl.when((ph == 1) & (j == n_steps - 1))
    def _():
        out = jnp.zeros((n_new, WIDTH), F32)
        for h in range(N_HEADS):
            rs = slice(h * n_new, (h + 1) * n_new)
            out = out + jnp.where(head_of_lane == h, acc_scr[rs, :] * (1.0 / l_scr[rs, 0:1]), 0.0)
        o_ref[...] = out


def _moba_sample(page_table, q, k_new, v_new, cache_kt, cache_vt, n_seq, n_new):
    n_pages = page_table.shape[1]
    past = n_pages * PAGE
    n_blk = past // MOBA_BLOCK
    pps = min(SAMPLE_PAGES_PER_STEP, n_pages)
    n_steps = n_pages // pps
    assert n_blk * MOBA_BLOCK == past and n_steps * pps == n_pages and n_new <= SUBLANES
    tok = lambda b, ph, j, pt: (b, 0)

    def kpage(pg):
        return lambda b, ph, j, pt: (pt[b, pps * jnp.where(ph == 0, j, n_steps - 1) + pg], 0, 0)

    def vpage(pg):
        return lambda b, ph, j, pt: (pt[b, pps * jnp.where(ph == 0, 0, j) + pg], 0, 0)

    page_block = (None, WIDTH, PAGE)
    rows = N_HEADS * n_new
    return pl.pallas_call(
        functools.partial(_moba_sample_kernel, n_blk=n_blk, n_new=n_new, past=past, pps=pps),
        grid_spec=pltpu.PrefetchScalarGridSpec(
            num_scalar_prefetch=1,
            grid=(n_seq, 2, n_steps),
            in_specs=[pl.BlockSpec((n_new, WIDTH), tok)] * 3
            + [pl.BlockSpec(page_block, kpage(pg)) for pg in range(pps)]
            + [pl.BlockSpec(page_block, vpage(pg)) for pg in range(pps)],
            out_specs=pl.BlockSpec((n_new, WIDTH), tok),
            scratch_shapes=[pltpu.VMEM((rows, WIDTH), F32),
                            pltpu.VMEM((rows, past), F32),
                            pltpu.VMEM((rows, LANES), F32),
                            pltpu.VMEM((rows, WIDTH), F32),
                            pltpu.VMEM((rows, LANES), F32)]),
        out_shape=jax.ShapeDtypeStruct((n_seq * n_new, WIDTH), F32),
        compiler_params=pltpu.CompilerParams(dimension_semantics=("arbitrary",) * 3, vmem_limit_bytes=VMEM_LIMIT),
        name="moba_sample",
    )(page_table, q, k_new, v_new, *([cache_kt] * pps), *([cache_vt] * pps))


def _gdn_kernel(xq_ref, xk_ref, xv_ref, z_ref, gb_ref, cwq_ref, cwk_ref, cwv_ref, cbq_ref, cbk_ref, cbv_ref,
                s0_ref, ng_ref, o_ref, s_out_ref, xs_scr, act_scr, gate_scr, pk_scr, q_scr, n_scr, dec_scr,
                *, seq, valid, row_block):
    hp = pl.program_id(1)
    c = GDN_CHUNK
    c2 = 2 * c
    pad = SUBLANES
    lane_r = lax.broadcasted_iota(jnp.int32, (row_block, PAIR), 1)
    half_mean = _group_mean_matrix(PAIR, HEAD_DIM)
    half_sum = _group_matrix(PAIR, HEAD_DIM, 1.0)

    er = lax.broadcasted_iota(jnp.int32, (LANES, LANES), 0)
    expand = [jnp.where(er == off + 2 * hp + a, 1.0, 0.0).astype(BF16) for off in (0, N_HEADS) for a in range(2)]

    for i, (x_ref, cb_ref) in enumerate(((xq_ref, cbq_ref), (xk_ref, cbk_ref), (xv_ref, cbv_ref))):
        xs_scr[i, pad - (CONV_W - 1):pad, :] = cb_ref[...]
        xs_scr[i, pad:pad + seq, :] = x_ref[...]
    for r0 in range(0, seq, row_block):
        live = (lax.broadcasted_iota(jnp.int32, (row_block, PAIR), 0) + r0) < valid
        for i, cw_ref in enumerate((cwq_ref, cwk_ref, cwv_ref)):
            conv = None
            for t in range(CONV_W):
                start = pad - (CONV_W - 1) + t + r0
                term = xs_scr[i, start:start + row_block, :] * cw_ref[t:t + 1, :]
                conv = term if conv is None else conv + term
            act = _silu(conv)
            if i < 2:
                ss = _dot_exact_rhs(act * act, half_sum)
                act = act * lax.rsqrt(ss + EPS)
            act_scr[i, r0:r0 + row_block, :] = jnp.where(live, act, 0.0)
        gb = gb_ref[r0:r0 + row_block, :]
        for e in range(4):
            gate_scr[e, r0:r0 + row_block, :] = jnp.where(live, _dot_exact_rhs(gb, expand[e], terms=3), 0.0)

    ri = lax.broadcasted_iota(jnp.int32, (c2, c2), 0)
    ci = lax.broadcasted_iota(jnp.int32, (c2, c2), 1)
    delta = jnp.where((ri // c) == (ci // c), ri - ci, -1)
    causal = delta >= 0
    strict = delta > 0
    tril = jnp.where(causal, 1.0, 0.0).astype(BF16)
    eye = jnp.where(ri == ci, 1.0, 0.0)
    lane_c = lax.broadcasted_iota(jnp.int32, (c, PAIR), 1)
    first = lane_c < HEAD_DIM
    diag_blocks = (lax.broadcasted_iota(jnp.int32, (PAIR, PAIR), 0) // HEAD_DIM) == (
        lax.broadcasted_iota(jnp.int32, (PAIR, PAIR), 1) // HEAD_DIM)
    scale = HEAD_DIM ** -0.5

    def stack(x):
        return jnp.concatenate([jnp.where(first, x, 0.0), jnp.where(first, 0.0, x)], axis=0)

    def unstack(x):
        return x[:c] + x[c:]

    zeros = jnp.zeros((HEAD_DIM, HEAD_DIM), F32)
    s_init = jnp.concatenate([jnp.concatenate([s0_ref[0], zeros], axis=1),
                              jnp.concatenate([zeros, s0_ref[1]], axis=1)], axis=0)

    def each(fn, *lists):
        return [fn(*args) for args in zip(*lists)]

    def prepare(ics):
        r0s = [pl.multiple_of(ic * c, c) for ic in ics]
        qc = [act_scr[0, pl.ds(r0, c), :] * scale for r0 in r0s]
        kc = [act_scr[1, pl.ds(r0, c), :] for r0 in r0s]
        vc = [act_scr[2, pl.ds(r0, c), :] for r0 in r0s]
        g_st = [jnp.concatenate([gate_scr[0, pl.ds(r0, c), :], gate_scr[1, pl.ds(r0, c), :]], axis=0) for r0 in r0s]
        b_st = [jnp.concatenate([gate_scr[2, pl.ds(r0, c), :], gate_scr[3, pl.ds(r0, c), :]], axis=0) for r0 in r0s]
        gc_st = each(lambda g: _dot_exact_lhs(tril, g, terms=3), g_st)
        decay = each(lambda g: jnp.where(causal, jnp.exp(jnp.minimum(g - g.T, 0.0)), 0.0), gc_st)
        k_st = each(stack, kc)
        kb_st = each(lambda k, b: k * b, k_st, b_st)
        k_stb = each(lambda k: k.astype(BF16), k_st)
        a_mat = each(lambda kb, k, d: jnp.where(strict, _dot(kb.astype(BF16), k, NT) * d, 0.0), kb_st, k_stb, decay)
        qk = each(lambda q, k, d: jnp.where(causal, _dot(stack(q).astype(BF16), k, NT) * d, 0.0).astype(BF16),
                  qc, k_stb, decay)
        base = each(lambda a: jnp.where(ri // SUBLANES == ci // SUBLANES, a, 0.0), a_mat)
        b2 = each(lambda b: _dot3(b, b), base)
        t_mat = each(lambda b: eye - b, base)
        t_mat = each(lambda t, x: t + _dot3(t, x), t_mat, b2)
        b4 = each(lambda x: _dot3(x, x), b2)
        t_mat = each(lambda t, x: t + _dot3(t, x), t_mat, b4)
        size = SUBLANES
        while size < c:
            off = each(lambda a: jnp.where(((ri // size) ^ (ci // size)) == 1, a, 0.0), a_mat)
            tm = each(_dot3, t_mat, off)
            t_mat = each(lambda t, x: t - _dot3(x, t), t_mat, tm)
            size *= 2
        rhs = each(lambda v, b, kb, g: jnp.concatenate([stack(v) * b, kb * jnp.exp(g)], axis=1).astype(BF16),
                   vc, b_st, kb_st, gc_st)
        uw = each(lambda t, r: _dot(t.astype(BF16), r), t_mat, rhs)
        u2 = each(lambda x: unstack(x[:, :PAIR]), uw)
        w2 = each(lambda x: unstack(x[:, PAIR:]), uw)
        gc2 = each(lambda g: jnp.where(first, g[:c], g[c:]), gc_st)
        g_last = each(lambda g: g[c - 1:c, :], gc2)
        kdt = each(lambda k, gl, g: (k * jnp.exp(gl - g)).T.astype(BF16), kc, g_last, gc2)
        kn = each(lambda kt, w, u: _dot(kt, jnp.concatenate([w, u], axis=1).astype(BF16)), kdt, w2, u2)
        qwu = each(lambda m, w, u: _dot(m, jnp.concatenate([stack(w), stack(u)], axis=1).astype(BF16)), qk, w2, u2)
        for i, ic in enumerate(ics):
            p_mat = qc[i] * jnp.exp(gc2[i]) - unstack(qwu[i][:, :PAIR])
            k_mat = jnp.where(diag_blocks, kn[i][:, :PAIR], 0.0)
            pk_scr[ic] = jnp.concatenate([p_mat, k_mat], axis=0).astype(BF16)
            q_scr[ic] = unstack(qwu[i][:, PAIR:])
            n_scr[ic] = jnp.where(diag_blocks, kn[i][:, PAIR:], 0.0)
            dec_scr[ic] = jnp.broadcast_to(jnp.exp(g_last[i]), (SUBLANES, PAIR))

    n_chunks = seq // c
    group = GDN_GROUP if n_chunks % GDN_GROUP == 0 else 1

    def prepare_group(i, carry):
        prepare([group * i + j for j in range(group)])
        return carry

    lax.fori_loop(0, n_chunks // group, prepare_group, 0)

    def recur(ic, s2):
        r0 = pl.multiple_of(ic * c, c)
        ps = _dot(pk_scr[ic], s2.astype(BF16))
        o2 = ps[:c] + q_scr[ic]
        s2 = s2 * dec_scr[ic, 0:1, :] - ps[c:] + n_scr[ic]
        ms = _dot_exact_rhs(o2 * o2, half_mean)
        zc = z_ref[pl.ds(r0, c), :]
        o_ref[pl.ds(r0, c), :] = o2 * lax.rsqrt(ms + EPS) * ng_ref[...] * _silu(zc)
        return s2

    s_fin = lax.fori_loop(0, n_chunks, recur, s_init)
    s_out_ref[0] = s_fin[:HEAD_DIM, :HEAD_DIM]
    s_out_ref[1] = s_fin[HEAD_DIM:, HEAD_DIM:]


def _gdn(gq, z, gb, conv_w, conv_buf, s0, ng, n_seq, seq, valid):
    row_block = min(seq, 256)
    n_chunks = seq // GDN_CHUNK
    seq_map = lambda off: (lambda b, hp: (b, 0, off + hp))
    cw_map = lambda off: (lambda b, hp: (0, off + hp))
    kern = functools.partial(_gdn_kernel, seq=seq, valid=valid, row_block=row_block)
    sb = pl.BlockSpec((None, seq, PAIR), seq_map(0))
    return pl.pallas_call(
        kern,
        grid=(n_seq, N_PAIRS),
        in_specs=[pl.BlockSpec((None, seq, PAIR), seq_map(0)), pl.BlockSpec((None, seq, PAIR), seq_map(N_PAIRS)),
                  pl.BlockSpec((None, seq, PAIR), seq_map(2 * N_PAIRS)), sb,
                  pl.BlockSpec((None, seq, LANES), lambda b, hp: (b, 0, 0)),
                  pl.BlockSpec((CONV_W, PAIR), cw_map(0)), pl.BlockSpec((CONV_W, PAIR), cw_map(N_PAIRS)),
                  pl.BlockSpec((CONV_W, PAIR), cw_map(2 * N_PAIRS)),
                  pl.BlockSpec((None, CONV_W - 1, PAIR), seq_map(0)),
                  pl.BlockSpec((None, CONV_W - 1, PAIR), seq_map(N_PAIRS)),
                  pl.BlockSpec((None, CONV_W - 1, PAIR), seq_map(2 * N_PAIRS)),
                  pl.BlockSpec((None, 2, HEAD_DIM, HEAD_DIM), lambda b, hp: (b, hp, 0, 0)),
                  pl.BlockSpec((1, PAIR), lambda b, hp: (0, 0))],
        out_specs=[sb, pl.BlockSpec((None, 2, HEAD_DIM, HEAD_DIM), lambda b, hp: (b, hp, 0, 0))],
        out_shape=[jax.ShapeDtypeStruct((n_seq, seq, WIDTH), F32),
                   jax.ShapeDtypeStruct((n_seq, N_HEADS, HEAD_DIM, HEAD_DIM), F32)],
        scratch_shapes=[pltpu.VMEM((3, seq + SUBLANES, PAIR), F32), pltpu.VMEM((3, seq, PAIR), F32),
                        pltpu.VMEM((4, seq, PAIR), F32),
                        pltpu.VMEM((n_chunks, GDN_CHUNK + PAIR, PAIR), BF16),
                        pltpu.VMEM((n_chunks, GDN_CHUNK, PAIR), F32),
                        pltpu.VMEM((n_chunks, PAIR, PAIR), F32),
                        pltpu.VMEM((n_chunks, SUBLANES, PAIR), F32)],
        compiler_params=pltpu.CompilerParams(dimension_semantics=("arbitrary",) * 2, vmem_limit_bytes=VMEM_LIMIT),
        name="gdn",
    )(gq, gq, gq, z, gb, conv_w, conv_w, conv_w, conv_buf, conv_buf, conv_buf, s0, ng)


def _outproj_kernel(x_ref, att_ref, gdn_ref, wo_ref, g2_ref, wr_ref, br_ref, h_ref, m_ref, ti_ref, tg_ref):
    mix = jnp.concatenate([att_ref[...], gdn_ref[...]], axis=1).astype(BF16)
    h = x_ref[...] + _dot(mix, wo_ref[...])
    h_ref[...] = h
    ms = jnp.mean(h * h, axis=-1, keepdims=True)
    m = h * lax.rsqrt(ms + EPS) * g2_ref[...]
    m_ref[...] = m
    logits = _dot3(m, wr_ref[...]) + br_ref[...]
    lane = lax.broadcasted_iota(jnp.int32, logits.shape, 1)
    ti = jnp.zeros(logits.shape, jnp.int32)
    tv = jnp.full(logits.shape, NEG, F32)
    for r in range(TOP_K):
        mx = jnp.max(logits, axis=1, keepdims=True)
        idx = jnp.min(jnp.where(logits == mx, lane, LANES), axis=1, keepdims=True)
        ti = jnp.where(lane == r, idx, ti)
        tv = jnp.where(lane == r, mx, tv)
        logits = jnp.where(lane == idx, -jnp.inf, logits)
    e = jnp.exp(tv - jnp.max(tv, axis=1, keepdims=True))
    ti_ref[...] = ti
    tg_ref[...] = e / jnp.sum(e, axis=1, keepdims=True)


def _outproj(x2, att, gdn, wo, g2, wr_pad, br_pad, tm):
    t = x2.shape[0]
    row = lambda i: (i, 0)
    fixed = lambda i: (0, 0)
    return pl.pallas_call(
        _outproj_kernel,
        grid=(t // tm,),
        in_specs=[pl.BlockSpec((tm, D_MODEL), row), pl.BlockSpec((tm, WIDTH), row), pl.BlockSpec((tm, WIDTH), row),
                  pl.BlockSpec((D_MODEL, D_MODEL), fixed), pl.BlockSpec((1, D_MODEL), fixed),
                  pl.BlockSpec((D_MODEL, LANES), fixed), pl.BlockSpec((1, LANES), fixed)],
        out_specs=[pl.BlockSpec((tm, D_MODEL), row), pl.BlockSpec((tm, D_MODEL), row),
                   pl.BlockSpec((tm, LANES), row), pl.BlockSpec((tm, LANES), row)],
        out_shape=[jax.ShapeDtypeStruct((t, D_MODEL), F32), jax.ShapeDtypeStruct((t, D_MODEL), F32),
                   jax.ShapeDtypeStruct((t, LANES), jnp.int32), jax.ShapeDtypeStruct((t, LANES), F32)],
        compiler_params=pltpu.CompilerParams(dimension_semantics=("arbitrary",), vmem_limit_bytes=VMEM_LIMIT),
        name="outproj",
    )(x2, att, gdn, wo, g2, wr_pad, br_pad)


def _for_each_assignment(tb, fn):
    def body(g, carry):
        base = pl.multiple_of(g * SUBLANES, SUBLANES)
        for j in range(SUBLANES):
            for k in range(TOP_K):
                fn(base + j, k)
        return carry

    lax.fori_loop(0, tb // SUBLANES, body, 0)


def _dispatch_kernel(dest_ref, m_ref, xin_ref, xbuf_ref, sem, *, tb):
    del xin_ref

    def row_copy(t, k):
        return pltpu.make_async_copy(m_ref.at[pl.ds(t, 1), :], xbuf_ref.at[pl.ds(dest_ref[t * TOP_K + k], 1), :], sem)

    _for_each_assignment(tb, lambda t, k: row_copy(t, k).start())
    _for_each_assignment(tb, lambda t, k: row_copy(t, k).wait())


def _dispatch(dest_flat, m3, xbuf, tb):
    t = m3.shape[0]
    return pl.pallas_call(
        functools.partial(_dispatch_kernel, tb=tb),
        grid=(t // tb,),
        in_specs=[pl.BlockSpec((tb * TOP_K,), lambda i: (i,), memory_space=pltpu.SMEM),
                  pl.BlockSpec((tb, D_MODEL), lambda i: (i, 0)),
                  pl.BlockSpec(memory_space=pl.ANY)],
        out_specs=pl.BlockSpec(memory_space=pl.ANY),
        out_shape=jax.ShapeDtypeStruct(xbuf.shape, xbuf.dtype),
        scratch_shapes=[pltpu.SemaphoreType.DMA(())],
        input_output_aliases={2: 0},
        compiler_params=pltpu.CompilerParams(dimension_semantics=("arbitrary",)),
        name="moe_dispatch",
    )(dest_flat, m3, xbuf)


def _split_gate_up_kernel(w_ref, wg_ref, wl_ref):
    n = 2 * LANES
    r = lax.broadcasted_iota(jnp.int32, (n, n), 0)
    c = lax.broadcasted_iota(jnp.int32, (n, n), 1)
    perm = jnp.where(r == jnp.where(c < LANES, 2 * c, 2 * (c - LANES) + 1), 1.0, 0.0).astype(BF16)
    for g in range(w_ref.shape[1] // n):
        y = _dot(w_ref[:, g * n:(g + 1) * n].astype(BF16), perm)
        wg_ref[:, g * LANES:(g + 1) * LANES] = y[:, :LANES].astype(BF16)
        wl_ref[:, g * LANES:(g + 1) * LANES] = y[:, LANES:].astype(BF16)


def _split_gate_up(w_gate_up):
    n_e, d_in, d_gu = w_gate_up.shape
    cols = 4 * LANES
    out = jax.ShapeDtypeStruct((n_e, d_in, d_gu // 2), BF16)
    return pl.pallas_call(
        _split_gate_up_kernel,
        grid=(n_e, d_gu // cols),
        in_specs=[pl.BlockSpec((None, d_in, cols), lambda e, j: (e, 0, j))],
        out_specs=[pl.BlockSpec((None, d_in, cols // 2), lambda e, j: (e, 0, j))] * 2,
        out_shape=[out, out],
        compiler_params=pltpu.CompilerParams(dimension_semantics=("arbitrary",) * 2, vmem_limit_bytes=VMEM_LIMIT),
        name="moe_split_gate_up",
    )(w_gate_up)


def _experts_kernel(be_ref, nb_ref, x_ref, wg_ref, wl_ref, wd_ref, bg_ref, bl_ref, bd_ref, y_ref):
    i = pl.program_id(0)

    @pl.when(i < nb_ref[0])
    def _():
        x = x_ref[...].astype(BF16)
        gate = jnp.minimum(_dot(x, wg_ref[...]) + bg_ref[...], SWIGLU_LIMIT)
        lin = jnp.clip(_dot(x, wl_ref[...]) + bl_ref[...], -SWIGLU_LIMIT, SWIGLU_LIMIT)
        hmid = gate * _sigmoid(SWIGLU_ALPHA * gate) * (lin + 1.0)
        y_ref[...] = _dot(hmid.astype(BF16), wd_ref[...]) + bd_ref[...]

    @pl.when(i >= nb_ref[0])
    def _():
        y_ref[...] = jnp.zeros(y_ref.shape, y_ref.dtype)


def _experts(block_e, nb_used, xbuf, wg, wl, wd, bg, bl, bd):
    n_rows = xbuf.shape[0]
    d_ff = wg.shape[2]
    xmap = lambda i, be, nb: (jnp.minimum(i, nb[0] - 1), 0)
    wmap = lambda i, be, nb: (be[i], 0, 0)
    return pl.pallas_call(
        _experts_kernel,
        grid_spec=pltpu.PrefetchScalarGridSpec(
            num_scalar_prefetch=2,
            grid=(n_rows // EXPERT_ROWS,),
            in_specs=[pl.BlockSpec((EXPERT_ROWS, D_MODEL), xmap),
                      pl.BlockSpec((None, D_MODEL, d_ff), wmap), pl.BlockSpec((None, D_MODEL, d_ff), wmap),
                      pl.BlockSpec((None, d_ff, D_MODEL), wmap),
                      pl.BlockSpec((None, 1, d_ff), wmap), pl.BlockSpec((None, 1, d_ff), wmap),
                      pl.BlockSpec((None, 1, D_MODEL), wmap)],
            out_specs=pl.BlockSpec((EXPERT_ROWS, D_MODEL), lambda i, be, nb: (i, 0))),
        out_shape=jax.ShapeDtypeStruct(xbuf.shape, F32),
        compiler_params=pltpu.CompilerParams(dimension_semantics=("arbitrary",), vmem_limit_bytes=VMEM_LIMIT),
        name="moe_experts",
    )(block_e, nb_used, xbuf, wg, wl, wd, bg, bl, bd)


def _combine_kernel(dest_ref, h_ref, tg_ref, y_ref, o_ref, buf, sem, *, tb):
    def row_copy(t, k):
        return pltpu.make_async_copy(y_ref.at[pl.ds(dest_ref[t * TOP_K + k], 1), :], buf.at[k, pl.ds(t, 1), :], sem)

    _for_each_assignment(tb, lambda t, k: row_copy(t, k).start())
    _for_each_assignment(tb, lambda t, k: row_copy(t, k).wait())
    tg = tg_ref[...]
    acc = h_ref[...]
    for k in range(TOP_K):
        acc = acc + tg[:, k:k + 1] * buf[k]
    o_ref[...] = acc


def _combine(dest_flat, h, tg, ybuf, tb):
    t = h.shape[0]
    row = lambda i: (i, 0)
    return pl.pallas_call(
        functools.partial(_combine_kernel, tb=tb),
        grid=(t // tb,),
        in_specs=[pl.BlockSpec((tb * TOP_K,), lambda i: (i,), memory_space=pltpu.SMEM),
                  pl.BlockSpec((tb, D_MODEL), row), pl.BlockSpec((tb, LANES), row),
                  pl.BlockSpec(memory_space=pl.ANY)],
        out_specs=pl.BlockSpec((tb, D_MODEL), row),
        out_shape=jax.ShapeDtypeStruct((t, D_MODEL), F32),
        scratch_shapes=[pltpu.VMEM((TOP_K, tb, D_MODEL), F32), pltpu.SemaphoreType.DMA(())],
        compiler_params=pltpu.CompilerParams(dimension_semantics=("arbitrary",), vmem_limit_bytes=VMEM_LIMIT),
        name="moe_combine",
    )(dest_flat, h, tg, ybuf)


def _routing(flat_e, n_blocks):
    onehot = (flat_e[:, None] == jnp.arange(N_EXPERTS, dtype=jnp.int32)[None, :]).astype(jnp.int32)
    csum = jnp.cumsum(onehot, axis=0)
    rank = jnp.sum(onehot * csum, axis=1) - 1
    counts = csum[-1]
    padded = (counts + EXPERT_ROWS - 1) // EXPERT_ROWS * EXPERT_ROWS
    pad_end = jnp.cumsum(padded)
    pad_start = pad_end - padded
    dest = (pad_start[flat_e] + rank).astype(jnp.int32)
    block_start = jnp.arange(n_blocks, dtype=jnp.int32) * EXPERT_ROWS
    block_e = jnp.minimum(jnp.sum((pad_end[None, :] <= block_start[:, None]).astype(jnp.int32), axis=1),
                          N_EXPERTS - 1).astype(jnp.int32)
    nb_used = (pad_end[-1:] // EXPERT_ROWS).astype(jnp.int32)
    return dest, block_e, nb_used


def kernel(x_prompt, x_sample, cache_k, cache_v, page_table, state_gdn, state_conv, norm1_g, w_in, q_norm_g,
           k_norm_g, conv_w, a_log, dt_bias, gdn_norm_g, w_out, norm2_g, w_router, b_router, w_gate_up,
           b_gate_up, w_down, b_down):
    assert w_in.shape[0] == 1, "single layer"
    b, s, d = x_prompt.shape
    db, dl, _ = x_sample.shape
    tp, ts = b * s, db * dl

    w_in_p = jnp.pad(w_in[0], ((0, 0), (0, IN_COLS_PAD - IN_COLS))).astype(BF16)
    qg = jnp.tile(q_norm_g[0], N_HEADS)[None, :]
    kg = jnp.tile(k_norm_g[0], N_HEADS)[None, :]
    neg_a = jnp.pad(-jnp.exp(a_log[0]), (0, LANES - N_HEADS))[None, :]
    dtb = jnp.pad(dt_bias[0], (0, LANES - N_HEADS))[None, :]
    ng = jnp.tile(gdn_norm_g[0], 2)[None, :]
    wo = w_out[0].astype(BF16)
    wr = jnp.pad(w_router[0], ((0, 0), (0, LANES - N_EXPERTS)))
    br = jnp.pad(b_router[0], (0, LANES - N_EXPERTS), constant_values=NEG)[None, :]
    wg, wl = _split_gate_up(w_gate_up[0])
    bg = b_gate_up[0][:, None, 0::2]
    bl = b_gate_up[0][:, None, 1::2]
    wd = w_down[0].astype(BF16)
    bd = b_down[0][:, None, :]
    slopes = (2.0 ** (-8.0 * jnp.arange(1, N_HEADS + 1, dtype=F32) / N_HEADS)).astype(F32)

    xp2 = x_prompt.reshape(tp, d)
    xs2 = x_sample.reshape(ts, d)
    g1 = norm1_g[0][None, :]
    g2 = norm2_g[0][None, :]

    qp, kp, vp, gqp, zp, gbp = _inproj(xp2, g1, w_in_p, qg, kg, neg_a, dtb, tm=512)
    att_p = _moba_prompt(slopes, qp, kp, vp, b, s)
    gdn_p, gstate_p = _gdn(gqp.reshape(b, s, GDN_QKV), zp.reshape(b, s, WIDTH), gbp.reshape(b, s, LANES),
                           conv_w[0], jnp.zeros((b, CONV_W - 1, GDN_QKV), F32),
                           jnp.zeros((b, N_HEADS, HEAD_DIM, HEAD_DIM), F32), ng, b, s, s)

    qs, ks, vs, gqs, zs, gbs = _inproj(xs2, g1, w_in_p, qg, kg, neg_a, dtb, tm=ts)
    page_major = lambda c: jnp.transpose(c[0], (0, 2, 3, 1)).reshape(c.shape[1], WIDTH, PAGE)
    att_s = _moba_sample(page_table, qs, ks, vs, page_major(cache_k), page_major(cache_v), db, dl)
    padl = lambda a: jnp.pad(a.reshape(db, dl, -1), ((0, 0), (0, GDN_CHUNK - dl), (0, 0)))
    gdn_s, gstate_s = _gdn(padl(gqs), padl(zs), padl(gbs), conv_w[0], state_conv[0], state_gdn[0], ng,
                           db, GDN_CHUNK, dl)
    gdn_s = gdn_s[:, :dl].reshape(ts, WIDTH)

    hp, mp, tip, tgp = _outproj(xp2, att_p, gdn_p.reshape(tp, WIDTH), wo, g2, wr, br, tm=512)
    hs, msm, tis, tgs = _outproj(xs2, att_s, gdn_s, wo, g2, wr, br, tm=ts)

    n_assign = (tp + ts) * TOP_K
    n_blocks = (n_assign + N_EXPERTS * (EXPERT_ROWS - 1)) // EXPERT_ROWS + 1
    flat_e = jnp.concatenate([tip[:, :TOP_K].reshape(-1), tis[:, :TOP_K].reshape(-1)])
    dest, block_e, nb_used = _routing(flat_e, n_blocks)
    dest_p, dest_s = dest[:tp * TOP_K], dest[tp * TOP_K:]
    xbuf = jnp.zeros((n_blocks * EXPERT_ROWS, D_MODEL), F32)
    xbuf = _dispatch(dest_p, mp, xbuf, tb=256)
    xbuf = _dispatch(dest_s, msm, xbuf, tb=ts)
    ybuf = _experts(block_e, nb_used, xbuf, wg, wl, wd, bg, bl, bd)
    y_prompt = _combine(dest_p, hp, tgp, ybuf, tb=256).reshape(b, s, d)
    y_sample = _combine(dest_s, hs, tgs, ybuf, tb=ts).reshape(db, dl, d)

    k_prompt = kp.reshape(1, b, s // PAGE, PAGE, N_HEADS, HEAD_DIM)
    v_prompt = vp.reshape(1, b, s // PAGE, PAGE, N_HEADS, HEAD_DIM)
    k_sample = ks.reshape(1, db, dl, N_HEADS, HEAD_DIM)
    v_sample = vs.reshape(1, db, dl, N_HEADS, HEAD_DIM)
    conv_prompt = gqp.reshape(b, s, GDN_QKV)[:, s - (CONV_W - 1):][None]
    conv_sample = jnp.concatenate([state_conv[0], gqs.reshape(db, dl, GDN_QKV)], axis=1)[:, -(CONV_W - 1):][None]
    return (y_prompt, y_sample, k_prompt, v_prompt, k_sample, v_sample,
            gstate_p[None], conv_prompt, gstate_s[None], conv_sample)
```

```python
import functools

import jax
import jax.numpy as jnp
import numpy as np
from jax import lax
from jax.experimental import pallas as pl
from jax.experimental.pallas import tpu as pltpu

F32 = jnp.float32
BF16 = jnp.bfloat16

LANES = 128
SUBLANES = 8
HEAD_DIM = 64
N_HEADS = 8
PAIR = 2 * HEAD_DIM
N_PAIRS = N_HEADS // 2
WIDTH = N_HEADS * HEAD_DIM
D_MODEL = 1024
GDN_QKV = 3 * WIDTH
IN_COLS = 3 * WIDTH + GDN_QKV + WIDTH + 2 * N_HEADS
IN_COLS_PAD = 3712
OFF_GQ = 3 * WIDTH
OFF_Z = OFF_GQ + GDN_QKV
OFF_AB = OFF_Z + WIDTH
MOBA_BLOCK = 256
MOBA_TOPK = 3
Q_CHUNK = 128
PAGE = 128
CONV_W = 4
GDN_CHUNK = 64
N_EXPERTS = 32
TOP_K = 4
SWIGLU_LIMIT = 7.0
SWIGLU_ALPHA = 1.702
EXPERT_ROWS = 256
SAMPLE_PAGES_PER_STEP = 16
SAMPLE_UNROLL = 4
GDN_GROUP = 8
EPS = 1e-6
NEG = -1e30
VMEM_LIMIT = 56 * 1024 * 1024

NN = (((1,), (0,)), ((), ()))
NT = (((1,), (1,)), ((), ()))


def _dot(a, b, dims=NN):
    return lax.dot_general(a, b, dims, preferred_element_type=F32)


def _split(x):
    hi = x.astype(BF16)
    lo = (x - hi.astype(F32)).astype(BF16)
    return hi, lo


def _dot3(a, b, dims=NN):
    ah, al = _split(a)
    bh, bl = _split(b)
    return _dot(ah, bh, dims) + (_dot(ah, bl, dims) + _dot(al, bh, dims))


def _dot_exact_rhs(a, b_bf16, terms=2):
    out = None
    r = a
    for _ in range(terms):
        h = r.astype(BF16)
        part = _dot(h, b_bf16)
        out = part if out is None else out + part
        r = r - h.astype(F32)
    return out


def _dot_exact_lhs(a_bf16, b, terms=2):
    out = None
    r = b
    for _ in range(terms):
        h = r.astype(BF16)
        part = _dot(a_bf16, h)
        out = part if out is None else out + part
        r = r - h.astype(F32)
    return out


def _group_matrix(n, group, value):
    r = lax.broadcasted_iota(jnp.int32, (n, n), 0) // group
    c = lax.broadcasted_iota(jnp.int32, (n, n), 1) // group
    return jnp.where(r == c, value, 0.0).astype(BF16)


def _group_mean_matrix(n, group):
    return _group_matrix(n, group, 1.0 / group)


def _sigmoid(x):
    return 1.0 / (1.0 + jnp.exp(-x))


def _silu(x):
    return x * _sigmoid(x)


def _inproj_kernel(x_ref, g1_ref, w_ref, qg_ref, kg_ref, neg_a_ref, dtb_ref,
                   q_ref, k_ref, v_ref, gq_ref, z_ref, gb_ref):
    x = x_ref[...]
    ms = jnp.mean(x * x, axis=-1, keepdims=True)
    n = x * lax.rsqrt(ms + EPS) * g1_ref[...]
    p = _dot(n.astype(BF16), w_ref[...])
    gm = _group_mean_matrix(2 * LANES, HEAD_DIM)

    def head_norm(t, gain):
        sq = t * t
        parts = [_dot_exact_rhs(sq[:, c:c + 2 * LANES], gm) for c in range(0, WIDTH, 2 * LANES)]
        return t * lax.rsqrt(jnp.concatenate(parts, axis=1) + EPS) * gain

    q_ref[...] = head_norm(p[:, 0:WIDTH], qg_ref[...])
    k_ref[...] = head_norm(p[:, WIDTH:2 * WIDTH], kg_ref[...])
    v_ref[...] = p[:, 2 * WIDTH:3 * WIDTH]
    gq_ref[...] = p[:, OFF_GQ:OFF_Z]
    z_ref[...] = p[:, OFF_Z:OFF_AB]
    ab = p[:, OFF_AB:IN_COLS_PAD]
    t = ab + dtb_ref[...]
    softplus = jnp.maximum(t, 0.0) + jnp.log(1.0 + jnp.exp(-jnp.abs(t)))
    lane = lax.broadcasted_iota(jnp.int32, ab.shape, 1)
    gb_ref[...] = jnp.where(lane < N_HEADS, neg_a_ref[...] * softplus, _sigmoid(ab))


def _inproj(x2, g1, w_pad, qg, kg, neg_a, dtb, tm):
    t = x2.shape[0]
    row = lambda i: (i, 0)
    fixed = lambda i: (0, 0)
    outs = [jax.ShapeDtypeStruct((t, WIDTH), F32)] * 3 + [
        jax.ShapeDtypeStruct((t, GDN_QKV), F32), jax.ShapeDtypeStruct((t, WIDTH), F32),
        jax.ShapeDtypeStruct((t, LANES), F32)]
    return pl.pallas_call(
        _inproj_kernel,
        grid=(t // tm,),
        in_specs=[pl.BlockSpec((tm, D_MODEL), row), pl.BlockSpec((1, D_MODEL), fixed),
                  pl.BlockSpec((D_MODEL, IN_COLS_PAD), fixed), pl.BlockSpec((1, WIDTH), fixed),
                  pl.BlockSpec((1, WIDTH), fixed), pl.BlockSpec((1, LANES), fixed),
                  pl.BlockSpec((1, LANES), fixed)],
        out_specs=[pl.BlockSpec((tm, WIDTH), row)] * 3 + [
            pl.BlockSpec((tm, GDN_QKV), row), pl.BlockSpec((tm, WIDTH), row), pl.BlockSpec((tm, LANES), row)],
        out_shape=outs,
        compiler_params=pltpu.CompilerParams(dimension_semantics=("arbitrary",), vmem_limit_bytes=VMEM_LIMIT),
        name="inproj",
    )(x2, g1, w_pad, qg, kg, neg_a, dtb)


def _moba_prompt_kernel(slopes_ref, q_ref, k_ref, v_ref, o_ref, kmean_ref, vt_ref, acc_ref, *, n_blocks):
    hp = pl.program_id(1)
    ob = pl.program_id(2)

    @pl.when(ob == 0)
    def _():
        for n in range(n_blocks):
            blk = slice(n * MOBA_BLOCK, (n + 1) * MOBA_BLOCK)
            kmean_ref[n:n + 1, :] = jnp.mean(k_ref[blk, :], axis=0, keepdims=True)
            vt_ref[:, blk] = v_ref[blk, :].T.astype(BF16)

    bq = MOBA_BLOCK
    cols = 2 * bq
    q = q_ref[...]
    lane = lax.broadcasted_iota(jnp.int32, (bq, PAIR), 1)
    qst = jnp.concatenate([jnp.where(lane < HEAD_DIM, q, 0.0), jnp.where(lane < HEAD_DIM, 0.0, q)], axis=0)
    qst_b = (qst * HEAD_DIM ** -0.5).astype(BF16)
    bidx = lax.broadcasted_iota(jnp.int32, (n_blocks, cols), 0)
    gate = jnp.where(bidx < ob, _dot3(kmean_ref[...], qst, NT), -jnp.inf)
    valid = jnp.where(bidx < ob, 1.0, 0.0)
    sel = jnp.zeros((n_blocks, cols), F32)
    for _ in range(min(MOBA_TOPK, n_blocks - 1)):
        mx = jnp.max(gate, axis=0, keepdims=True)
        idx = jnp.min(jnp.where(gate == mx, bidx, n_blocks), axis=0, keepdims=True)
        hit = bidx == idx
        sel = jnp.where(hit, valid, sel)
        gate = jnp.where(hit, -jnp.inf, gate)

    col = lax.broadcasted_iota(jnp.int32, (1, cols), 1)
    slope = jnp.where(col < bq, slopes_ref[2 * hp], slopes_ref[2 * hp + 1])
    rel = (lax.broadcasted_iota(jnp.int32, (bq, cols), 1) % bq
           - lax.broadcasted_iota(jnp.int32, (bq, cols), 0)).astype(F32)
    alibi = slope * rel

    def logits(n):
        start = pl.multiple_of(n * MOBA_BLOCK, MOBA_BLOCK)
        kb = k_ref[pl.ds(start, MOBA_BLOCK), :].astype(BF16)
        return _dot(kb, qst_b, NT) - alibi, start

    s, start = logits(ob)
    s = jnp.where(rel >= 0, s, NEG)
    m0 = jnp.max(s, axis=0, keepdims=True)
    p = jnp.exp(s - m0)
    l0 = jnp.sum(p, axis=0, keepdims=True)
    acc_ref[...] = _dot(vt_ref[:, pl.ds(start, MOBA_BLOCK)], p.astype(BF16))

    def body(pair, carry):
        m_i, l_i = carry
        n = 2 * pair
        start = pl.multiple_of(n * MOBA_BLOCK, 2 * MOBA_BLOCK)
        raw = _dot(k_ref[pl.ds(start, 2 * MOBA_BLOCK), :].astype(BF16), qst_b, NT)
        far = slope * jnp.full((1, cols), (ob - n) * MOBA_BLOCK, jnp.int32).astype(F32)
        halves = []
        for i in range(2):
            sel_i = jnp.sum(jnp.where(bidx == n + i, sel, 0.0), axis=0, keepdims=True)
            s = raw[i * MOBA_BLOCK:(i + 1) * MOBA_BLOCK] - alibi - (far - slope * float(i * MOBA_BLOCK))
            halves.append(jnp.where(sel_i > 0, s, NEG))
        m_new = jnp.maximum(m_i, jnp.maximum(jnp.max(halves[0], axis=0, keepdims=True),
                                             jnp.max(halves[1], axis=0, keepdims=True)))
        alpha = jnp.exp(m_i - m_new)
        p = [jnp.exp(s - m_new) for s in halves]
        pv = _dot(vt_ref[:, pl.ds(start, 2 * MOBA_BLOCK)], jnp.concatenate(p, axis=0).astype(BF16))
        acc_ref[...] = acc_ref[...] * alpha + pv
        return m_new, l_i * alpha + (jnp.sum(p[0], axis=0, keepdims=True) + jnp.sum(p[1], axis=0, keepdims=True))

    _, l_f = lax.fori_loop(0, (ob + 1) // 2, body, (m0, l0))
    o_t = acc_ref[...] * (1.0 / l_f)
    o_ref[...] = jnp.concatenate([o_t[:HEAD_DIM, :bq], o_t[HEAD_DIM:, bq:]], axis=0).T


def _moba_prompt(slopes, q, k, v, batch, seq):
    n_blocks = seq // MOBA_BLOCK
    qmap = lambda b, hp, ob: (b * n_blocks + ob, hp)
    kmap = lambda b, hp, ob: (b, hp)
    return pl.pallas_call(
        functools.partial(_moba_prompt_kernel, n_blocks=n_blocks),
        grid=(batch, N_PAIRS, n_blocks),
        in_specs=[pl.BlockSpec(memory_space=pltpu.SMEM),
                  pl.BlockSpec((MOBA_BLOCK, PAIR), qmap), pl.BlockSpec((seq, PAIR), kmap),
                  pl.BlockSpec((seq, PAIR), kmap)],
        out_specs=pl.BlockSpec((MOBA_BLOCK, PAIR), qmap),
        out_shape=jax.ShapeDtypeStruct((batch * seq, WIDTH), F32),
        scratch_shapes=[pltpu.VMEM((n_blocks, PAIR), F32), pltpu.VMEM((PAIR, seq), BF16),
                        pltpu.VMEM((PAIR, 2 * MOBA_BLOCK), F32)],
        compiler_params=pltpu.CompilerParams(dimension_semantics=("arbitrary",) * 3, vmem_limit_bytes=VMEM_LIMIT),
        name="moba_prompt",
    )(slopes, q, k, v)


def _moba_sample_kernel(pt_ref, q_ref, kn_ref, vn_ref, *refs, n_blk, n_new, past, pps):
    k_refs, v_refs = refs[:pps], refs[pps:2 * pps]
    o_ref, q_scr, s_scr, l_scr, acc_scr, gate_scr = refs[2 * pps:]
    ph = pl.program_id(1)
    j = pl.program_id(2)
    rows = N_HEADS * n_new
    scale = HEAD_DIM ** -0.5
    n_steps = (past // PAGE) // pps
    head_of_lane = lax.broadcasted_iota(jnp.int32, (n_new, WIDTH), 1) // HEAD_DIM

    @pl.when((ph == 0) & (j == 0))
    def _():
        q = q_ref[...]
        for h in range(N_HEADS):
            q_scr[h * n_new:(h + 1) * n_new, :] = jnp.where(head_of_lane == h, q, 0.0)

        gate_scr[...] = jnp.full((rows, LANES), -jnp.inf, F32)

    ppb = MOBA_BLOCK // PAGE
    lane = lax.broadcasted_iota(jnp.int32, (rows, LANES), 1)

    @pl.when(ph == 0)
    def _():
        qb = q_scr[...].astype(BF16)
        gate = gate_scr[...]
        for blk in range(pps // ppb):
            total = None
            for pg in range(blk * ppb, (blk + 1) * ppb):
                col = pl.multiple_of((j * pps + pg) * PAGE, PAGE)
                s = _dot(qb, k_refs[pg][...].astype(BF16))
                s_scr[:, pl.ds(col, PAGE)] = s
                total = s if total is None else total + s
            mean = jnp.sum(total, axis=1, keepdims=True) * (1.0 / MOBA_BLOCK)
            gate = jnp.where(lane == j * (pps // ppb) + blk, mean, gate)
        gate_scr[...] = gate

    slope_col = jnp.concatenate(
        [jnp.full((n_new, 1), 2.0 ** (-8.0 * (h + 1) / N_HEADS), F32) for h in range(N_HEADS)], axis=0)
    qpos = (past + lax.broadcasted_iota(jnp.int32, (rows, 1), 0) % n_new).astype(F32)

    def block_logits(n):
        start = pl.multiple_of(n * MOBA_BLOCK, MOBA_BLOCK)
        raw = s_scr[:, pl.ds(start, MOBA_BLOCK)]
        kpos = (n * MOBA_BLOCK + lax.broadcasted_iota(jnp.int32, (rows, MOBA_BLOCK), 1)).astype(F32)
        return raw * scale - slope_col * (qpos - kpos), start

    @pl.when((ph == 1) & (j == 0))
    def _():
        gate = gate_scr[...]
        sel = jnp.zeros((rows, LANES), F32)
        for _ in range(min(MOBA_TOPK, n_blk)):
            mx = jnp.max(gate, axis=1, keepdims=True)
            idx = jnp.min(jnp.where(gate == mx, lane, LANES), axis=1, keepdims=True)
            hit = lane == idx
            sel = jnp.where(hit, 1.0, sel)
            gate = jnp.where(hit, -jnp.inf, gate)

        own = _dot(q_scr[...], kn_ref[...], NT)
        li = lax.broadcasted_iota(jnp.int32, (rows, n_new), 0) % n_new
        ri = lax.broadcasted_iota(jnp.int32, (rows, n_new), 1)
        d_own = (li - ri).astype(F32)
        lg_own = jnp.where(d_own >= 0, own * scale - slope_col * d_own, NEG)
        m0 = jnp.max(lg_own, axis=1, keepdims=True)

        def masked_logits(n):
            lg, start = block_logits(n)
            picked = jnp.sum(jnp.where(lane == n, sel, 0.0), axis=1, keepdims=True) > 0
            return jnp.where(picked, lg, NEG), start

        def max_body(n, mv):
            lg, _ = masked_logits(n)
            return jnp.maximum(mv, jnp.maximum(lg[:, :LANES], lg[:, LANES:]))

        mv = lax.fori_loop(0, n_blk, max_body, jnp.full((rows, LANES), NEG, F32), unroll=SAMPLE_UNROLL)
        m = jnp.maximum(m0, jnp.max(mv, axis=1, keepdims=True))
        p_own = jnp.exp(lg_own - m)

        def p_body(n, lv):
            lg, start = masked_logits(n)
            p = jnp.exp(lg - m)
            s_scr[:, pl.ds(start, MOBA_BLOCK)] = p
            return lv + (p[:, :LANES] + p[:, LANES:])

        lv = lax.fori_loop(0, n_blk, p_body, jnp.zeros((rows, LANES), F32), unroll=SAMPLE_UNROLL)
        l = jnp.sum(p_own, axis=1, keepdims=True) + jnp.sum(lv, axis=1, keepdims=True)
        l_scr[...] = jnp.broadcast_to(l, (rows, LANES))
        acc_scr[...] = _dot(p_own, vn_ref[...])

    @pl.when(ph == 1)
    def _():
        acc = acc_scr[...]
        for pg in range(pps):
            col = pl.multiple_of((j * pps + pg) * PAGE, PAGE)
            acc = acc + _dot(s_scr[:, pl.ds(col, PAGE)].astype(BF16), v_refs[pg][...].astype(BF16), NT)
        acc_scr[...] = acc

    @pl.when((ph == 1) & (j == n_steps - 1))
    def _():
        out = jnp.zeros((n_new, WIDTH), F32)
        for h in range(N_HEADS):
            rs = slice(h * n_new, (h + 1) * n_new)
            out = out + jnp.where(head_of_lane == h, acc_scr[rs, :] * (1.0 / l_scr[rs, 0:1]), 0.0)
        o_ref[...] = out


def _moba_sample(page_table, q, k_new, v_new, cache_kt, cache_vt, n_seq, n_new):
    n_pages = page_table.shape[1]
    past = n_pages * PAGE
    n_blk = past // MOBA_BLOCK
    pps = min(SAMPLE_PAGES_PER_STEP, n_pages)
    n_steps = n_pages // pps
    assert n_blk * MOBA_BLOCK == past and n_steps * pps == n_pages and n_new <= SUBLANES
    tok = lambda b, ph, j, pt: (b, 0)

    def kpage(pg):
        return lambda b, ph, j, pt: (pt[b, pps * jnp.where(ph == 0, j, n_steps - 1) + pg], 0, 0)

    def vpage(pg):
        return lambda b, ph, j, pt: (pt[b, pps * jnp.where(ph == 0, 0, j) + pg], 0, 0)

    page_block = (None, WIDTH, PAGE)
    rows = N_HEADS * n_new
    return pl.pallas_call(
        functools.partial(_moba_sample_kernel, n_blk=n_blk, n_new=n_new, past=past, pps=pps),
        grid_spec=pltpu.PrefetchScalarGridSpec(
            num_scalar_prefetch=1,
            grid=(n_seq, 2, n_steps),
            in_specs=[pl.BlockSpec((n_new, WIDTH), tok)] * 3
            + [pl.BlockSpec(page_block, kpage(pg)) for pg in range(pps)]
            + [pl.BlockSpec(page_block, vpage(pg)) for pg in range(pps)],
            out_specs=pl.BlockSpec((n_new, WIDTH), tok),
            scratch_shapes=[pltpu.VMEM((rows, WIDTH), F32),
                            pltpu.VMEM((rows, past), F32),
                            pltpu.VMEM((rows, LANES), F32),
                            pltpu.VMEM((rows, WIDTH), F32),
                            pltpu.VMEM((rows, LANES), F32)]),
        out_shape=jax.ShapeDtypeStruct((n_seq * n_new, WIDTH), F32),
        compiler_params=pltpu.CompilerParams(dimension_semantics=("arbitrary",) * 3, vmem_limit_bytes=VMEM_LIMIT),
        name="moba_sample",
    )(page_table, q, k_new, v_new, *([cache_kt] * pps), *([cache_vt] * pps))


def _gdn_kernel(xq_ref, xk_ref, xv_ref, z_ref, gb_ref, cwq_ref, cwk_ref, cwv_ref, cbq_ref, cbk_ref, cbv_ref,
                s0_ref, ng_ref, o_ref, s_out_ref, xs_scr, act_scr, gate_scr, pk_scr, q_scr, n_scr, dec_scr,
                *, seq, valid, row_block):
    hp = pl.program_id(1)
    c = GDN_CHUNK
    c2 = 2 * c
    pad = SUBLANES
    lane_r = lax.broadcasted_iota(jnp.int32, (row_block, PAIR), 1)
    half_mean = _group_mean_matrix(PAIR, HEAD_DIM)
    half_sum = _group_matrix(PAIR, HEAD_DIM, 1.0)

    er = lax.broadcasted_iota(jnp.int32, (LANES, LANES), 0)
    expand = [jnp.where(er == off + 2 * hp + a, 1.0, 0.0).astype(BF16) for off in (0, N_HEADS) for a in range(2)]

    for i, (x_ref, cb_ref) in enumerate(((xq_ref, cbq_ref), (xk_ref, cbk_ref), (xv_ref, cbv_ref))):
        xs_scr[i, pad - (CONV_W - 1):pad, :] = cb_ref[...]
        xs_scr[i, pad:pad + seq, :] = x_ref[...]
    for r0 in range(0, seq, row_block):
        live = (lax.broadcasted_iota(jnp.int32, (row_block, PAIR), 0) + r0) < valid
        for i, cw_ref in enumerate((cwq_ref, cwk_ref, cwv_ref)):
            conv = None
            for t in range(CONV_W):
                start = pad - (CONV_W - 1) + t + r0
                term = xs_scr[i, start:start + row_block, :] * cw_ref[t:t + 1, :]
                conv = term if conv is None else conv + term
            act = _silu(conv)
            if i < 2:
                ss = _dot_exact_rhs(act * act, half_sum)
                act = act * lax.rsqrt(ss + EPS)
            act_scr[i, r0:r0 + row_block, :] = jnp.where(live, act, 0.0)
        gb = gb_ref[r0:r0 + row_block, :]
        for e in range(4):
            gate_scr[e, r0:r0 + row_block, :] = jnp.where(live, _dot_exact_rhs(gb, expand[e], terms=3), 0.0)

    ri = lax.broadcasted_iota(jnp.int32, (c2, c2), 0)
    ci = lax.broadcasted_iota(jnp.int32, (c2, c2), 1)
    delta = jnp.where((ri // c) == (ci // c), ri - ci, -1)
    causal = delta >= 0
    strict = delta > 0
    tril = jnp.where(causal, 1.0, 0.0).astype(BF16)
    eye = jnp.where(ri == ci, 1.0, 0.0)
    lane_c = lax.broadcasted_iota(jnp.int32, (c, PAIR), 1)
    first = lane_c < HEAD_DIM
    diag_blocks = (lax.broadcasted_iota(jnp.int32, (PAIR, PAIR), 0) // HEAD_DIM) == (
        lax.broadcasted_iota(jnp.int32, (PAIR, PAIR), 1) // HEAD_DIM)
    scale = HEAD_DIM ** -0.5

    def stack(x):
        return jnp.concatenate([jnp.where(first, x, 0.0), jnp.where(first, 0.0, x)], axis=0)

    def unstack(x):
        return x[:c] + x[c:]

    zeros = jnp.zeros((HEAD_DIM, HEAD_DIM), F32)
    s_init = jnp.concatenate([jnp.concatenate([s0_ref[0], zeros], axis=1),
                              jnp.concatenate([zeros, s0_ref[1]], axis=1)], axis=0)

    def each(fn, *lists):
        return [fn(*args) for args in zip(*lists)]

    def prepare(ics):
        r0s = [pl.multiple_of(ic * c, c) for ic in ics]
        qc = [act_scr[0, pl.ds(r0, c), :] * scale for r0 in r0s]
        kc = [act_scr[1, pl.ds(r0, c), :] for r0 in r0s]
        vc = [act_scr[2, pl.ds(r0, c), :] for r0 in r0s]
        g_st = [jnp.concatenate([gate_scr[0, pl.ds(r0, c), :], gate_scr[1, pl.ds(r0, c), :]], axis=0) for r0 in r0s]
        b_st = [jnp.concatenate([gate_scr[2, pl.ds(r0, c), :], gate_scr[3, pl.ds(r0, c), :]], axis=0) for r0 in r0s]
        gc_st = each(lambda g: _dot_exact_lhs(tril, g, terms=3), g_st)
        decay = each(lambda g: jnp.where(causal, jnp.exp(jnp.minimum(g - g.T, 0.0)), 0.0), gc_st)
        k_st = each(stack, kc)
        kb_st = each(lambda k, b: k * b, k_st, b_st)
        k_stb = each(lambda k: k.astype(BF16), k_st)
        a_mat = each(lambda kb, k, d: jnp.where(strict, _dot(kb.astype(BF16), k, NT) * d, 0.0), kb_st, k_stb, decay)
        qk = each(lambda q, k, d: jnp.where(causal, _dot(stack(q).astype(BF16), k, NT) * d, 0.0).astype(BF16),
                  qc, k_stb, decay)
        base = each(lambda a: jnp.where(ri // SUBLANES == ci // SUBLANES, a, 0.0), a_mat)
        b2 = each(lambda b: _dot3(b, b), base)
        t_mat = each(lambda b: eye - b, base)
        t_mat = each(lambda t, x: t + _dot3(t, x), t_mat, b2)
        b4 = each(lambda x: _dot3(x, x), b2)
        t_mat = each(lambda t, x: t + _dot3(t, x), t_mat, b4)
        size = SUBLANES
        while size < c:
            off = each(lambda a: jnp.where(((ri // size) ^ (ci // size)) == 1, a, 0.0), a_mat)
            t_b = each(lambda t: t.astype(BF16), t_mat)
            tm = each(lambda t, x: _dot(t, x.astype(BF16)), t_b, off)
            t_mat = each(lambda t, tb_, x: t - _dot(x.astype(BF16), tb_), t_mat, t_b, tm)
            size *= 2
        rhs = each(lambda v, b, kb, g: jnp.concatenate([stack(v) * b, kb * jnp.exp(g)], axis=1).astype(BF16),
                   vc, b_st, kb_st, gc_st)
        uw = each(lambda t, r: _dot(t.astype(BF16), r), t_mat, rhs)
        u2 = each(lambda x: unstack(x[:, :PAIR]), uw)
        w2 = each(lambda x: unstack(x[:, PAIR:]), uw)
        gc2 = each(lambda g: jnp.where(first, g[:c], g[c:]), gc_st)
        g_last = each(lambda g: g[c - 1:c, :], gc2)
        kdt = each(lambda k, gl, g: (k * jnp.exp(gl - g)).T.astype(BF16), kc, g_last, gc2)
        kn = each(lambda kt, w, u: _dot(kt, jnp.concatenate([w, u], axis=1).astype(BF16)), kdt, w2, u2)
        qwu = each(lambda m, w, u: _dot(m, jnp.concatenate([stack(w), stack(u)], axis=1).astype(BF16)), qk, w2, u2)
        for i, ic in enumerate(ics):
            p_mat = qc[i] * jnp.exp(gc2[i]) - unstack(qwu[i][:, :PAIR])
            k_mat = jnp.where(diag_blocks, kn[i][:, :PAIR], 0.0)
            pk_scr[ic] = jnp.concatenate([p_mat, k_mat], axis=0).astype(BF16)
            q_scr[ic] = unstack(qwu[i][:, PAIR:])
            n_scr[ic] = jnp.where(diag_blocks, kn[i][:, PAIR:], 0.0)
            dec_scr[ic] = jnp.broadcast_to(jnp.exp(g_last[i]), (SUBLANES, PAIR))

    n_chunks = seq // c
    group = GDN_GROUP if n_chunks % GDN_GROUP == 0 else 1

    def prepare_group(i, carry):
        prepare([group * i + j for j in range(group)])
        return carry

    lax.fori_loop(0, n_chunks // group, prepare_group, 0)

    def recur(ic, s2):
        r0 = pl.multiple_of(ic * c, c)
        ps = _dot(pk_scr[ic], s2.astype(BF16))
        o2 = ps[:c] + q_scr[ic]
        s2 = s2 * dec_scr[ic, 0:1, :] - ps[c:] + n_scr[ic]
        ms = _dot_exact_rhs(o2 * o2, half_mean)
        zc = z_ref[pl.ds(r0, c), :]
        o_ref[pl.ds(r0, c), :] = o2 * lax.rsqrt(ms + EPS) * ng_ref[...] * _silu(zc)
        return s2

    s_fin = lax.fori_loop(0, n_chunks, recur, s_init)
    s_out_ref[0] = s_fin[:HEAD_DIM, :HEAD_DIM]
    s_out_ref[1] = s_fin[HEAD_DIM:, HEAD_DIM:]


def _gdn(gq, z, gb, conv_w, conv_buf, s0, ng, n_seq, seq, valid):
    row_block = min(seq, 256)
    n_chunks = seq // GDN_CHUNK
    seq_map = lambda off: (lambda b, hp: (b, 0, off + hp))
    cw_map = lambda off: (lambda b, hp: (0, off + hp))
    kern = functools.partial(_gdn_kernel, seq=seq, valid=valid, row_block=row_block)
    sb = pl.BlockSpec((None, seq, PAIR), seq_map(0))
    return pl.pallas_call(
        kern,
        grid=(n_seq, N_PAIRS),
        in_specs=[pl.BlockSpec((None, seq, PAIR), seq_map(0)), pl.BlockSpec((None, seq, PAIR), seq_map(N_PAIRS)),
                  pl.BlockSpec((None, seq, PAIR), seq_map(2 * N_PAIRS)), sb,
                  pl.BlockSpec((None, seq, LANES), lambda b, hp: (b, 0, 0)),
                  pl.BlockSpec((CONV_W, PAIR), cw_map(0)), pl.BlockSpec((CONV_W, PAIR), cw_map(N_PAIRS)),
                  pl.BlockSpec((CONV_W, PAIR), cw_map(2 * N_PAIRS)),
                  pl.BlockSpec((None, CONV_W - 1, PAIR), seq_map(0)),
                  pl.BlockSpec((None, CONV_W - 1, PAIR), seq_map(N_PAIRS)),
                  pl.BlockSpec((None, CONV_W - 1, PAIR), seq_map(2 * N_PAIRS)),
                  pl.BlockSpec((None, 2, HEAD_DIM, HEAD_DIM), lambda b, hp: (b, hp, 0, 0)),
                  pl.BlockSpec((1, PAIR), lambda b, hp: (0, 0))],
        out_specs=[sb, pl.BlockSpec((None, 2, HEAD_DIM, HEAD_DIM), lambda b, hp: (b, hp, 0, 0))],
        out_shape=[jax.ShapeDtypeStruct((n_seq, seq, WIDTH), F32),
                   jax.ShapeDtypeStruct((n_seq, N_HEADS, HEAD_DIM, HEAD_DIM), F32)],
        scratch_shapes=[pltpu.VMEM((3, seq + SUBLANES, PAIR), F32), pltpu.VMEM((3, seq, PAIR), F32),
                        pltpu.VMEM((4, seq, PAIR), F32),
                        pltpu.VMEM((n_chunks, GDN_CHUNK + PAIR, PAIR), BF16),
                        pltpu.VMEM((n_chunks, GDN_CHUNK, PAIR), F32),
                        pltpu.VMEM((n_chunks, PAIR, PAIR), F32),
                        pltpu.VMEM((n_chunks, SUBLANES, PAIR), F32)],
        compiler_params=pltpu.CompilerParams(dimension_semantics=("arbitrary",) * 2, vmem_limit_bytes=VMEM_LIMIT),
        name="gdn",
    )(gq, gq, gq, z, gb, conv_w, conv_w, conv_w, conv_buf, conv_buf, conv_buf, s0, ng)


def _outproj_kernel(x_ref, att_ref, gdn_ref, wo_ref, g2_ref, wr_ref, br_ref, h_ref, m_ref, ti_ref, tg_ref):
    mix = jnp.concatenate([att_ref[...], gdn_ref[...]], axis=1).astype(BF16)
    h = x_ref[...] + _dot(mix, wo_ref[...])
    h_ref[...] = h
    ms = jnp.mean(h * h, axis=-1, keepdims=True)
    m = h * lax.rsqrt(ms + EPS) * g2_ref[...]
    m_ref[...] = m
    logits = _dot3(m, wr_ref[...]) + br_ref[...]
    lane = lax.broadcasted_iota(jnp.int32, logits.shape, 1)
    ti = jnp.zeros(logits.shape, jnp.int32)
    tv = jnp.full(logits.shape, NEG, F32)
    for r in range(TOP_K):
        mx = jnp.max(logits, axis=1, keepdims=True)
        idx = jnp.min(jnp.where(logits == mx, lane, LANES), axis=1, keepdims=True)
        ti = jnp.where(lane == r, idx, ti)
        tv = jnp.where(lane == r, mx, tv)
        logits = jnp.where(lane == idx, -jnp.inf, logits)
    e = jnp.exp(tv - jnp.max(tv, axis=1, keepdims=True))
    ti_ref[...] = ti
    tg_ref[...] = e / jnp.sum(e, axis=1, keepdims=True)


def _outproj(x2, att, gdn, wo, g2, wr_pad, br_pad, tm):
    t = x2.shape[0]
    row = lambda i: (i, 0)
    fixed = lambda i: (0, 0)
    return pl.pallas_call(
        _outproj_kernel,
        grid=(t // tm,),
        in_specs=[pl.BlockSpec((tm, D_MODEL), row), pl.BlockSpec((tm, WIDTH), row), pl.BlockSpec((tm, WIDTH), row),
                  pl.BlockSpec((D_MODEL, D_MODEL), fixed), pl.BlockSpec((1, D_MODEL), fixed),
                  pl.BlockSpec((D_MODEL, LANES), fixed), pl.BlockSpec((1, LANES), fixed)],
        out_specs=[pl.BlockSpec((tm, D_MODEL), row), pl.BlockSpec((tm, D_MODEL), row),
                   pl.BlockSpec((tm, LANES), row), pl.BlockSpec((tm, LANES), row)],
        out_shape=[jax.ShapeDtypeStruct((t, D_MODEL), F32), jax.ShapeDtypeStruct((t, D_MODEL), F32),
                   jax.ShapeDtypeStruct((t, LANES), jnp.int32), jax.ShapeDtypeStruct((t, LANES), F32)],
        compiler_params=pltpu.CompilerParams(dimension_semantics=("arbitrary",), vmem_limit_bytes=VMEM_LIMIT),
        name="outproj",
    )(x2, att, gdn, wo, g2, wr_pad, br_pad)


def _for_each_assignment(tb, fn):
    def body(g, carry):
        base = pl.multiple_of(g * SUBLANES, SUBLANES)
        for j in range(SUBLANES):
            for k in range(TOP_K):
                fn(base + j, k)
        return carry

    lax.fori_loop(0, tb // SUBLANES, body, 0)


def _dispatch_kernel(dest_ref, m_ref, xin_ref, xbuf_ref, sem, *, tb):
    del xin_ref

    def row_copy(t, k):
        return pltpu.make_async_copy(m_ref.at[pl.ds(t, 1), :], xbuf_ref.at[pl.ds(dest_ref[t * TOP_K + k], 1), :], sem)

    _for_each_assignment(tb, lambda t, k: row_copy(t, k).start())
    _for_each_assignment(tb, lambda t, k: row_copy(t, k).wait())


def _dispatch(dest_flat, m3, xbuf, tb):
    t = m3.shape[0]
    return pl.pallas_call(
        functools.partial(_dispatch_kernel, tb=tb),
        grid=(t // tb,),
        in_specs=[pl.BlockSpec((tb * TOP_K,), lambda i: (i,), memory_space=pltpu.SMEM),
                  pl.BlockSpec((tb, D_MODEL), lambda i: (i, 0)),
                  pl.BlockSpec(memory_space=pl.ANY)],
        out_specs=pl.BlockSpec(memory_space=pl.ANY),
        out_shape=jax.ShapeDtypeStruct(xbuf.shape, xbuf.dtype),
        scratch_shapes=[pltpu.SemaphoreType.DMA(())],
        input_output_aliases={2: 0},
        compiler_params=pltpu.CompilerParams(dimension_semantics=("arbitrary",)),
        name="moe_dispatch",
    )(dest_flat, m3, xbuf)


def _split_gate_up_kernel(w_ref, wg_ref, wl_ref):
    n = 2 * LANES
    r = lax.broadcasted_iota(jnp.int32, (n, n), 0)
    c = lax.broadcasted_iota(jnp.int32, (n, n), 1)
    perm = jnp.where(r == jnp.where(c < LANES, 2 * c, 2 * (c - LANES) + 1), 1.0, 0.0).astype(BF16)
    for g in range(w_ref.shape[1] // n):
        y = _dot(w_ref[:, g * n:(g + 1) * n].astype(BF16), perm)
        wg_ref[:, g * LANES:(g + 1) * LANES] = y[:, :LANES].astype(BF16)
        wl_ref[:, g * LANES:(g + 1) * LANES] = y[:, LANES:].astype(BF16)


def _split_gate_up(w_gate_up):
    n_e, d_in, d_gu = w_gate_up.shape
    cols = 4 * LANES
    out = jax.ShapeDtypeStruct((n_e, d_in, d_gu // 2), BF16)
    return pl.pallas_call(
        _split_gate_up_kernel,
        grid=(n_e, d_gu // cols),
        in_specs=[pl.BlockSpec((None, d_in, cols), lambda e, j: (e, 0, j))],
        out_specs=[pl.BlockSpec((None, d_in, cols // 2), lambda e, j: (e, 0, j))] * 2,
        out_shape=[out, out],
        compiler_params=pltpu.CompilerParams(dimension_semantics=("arbitrary",) * 2, vmem_limit_bytes=VMEM_LIMIT),
        name="moe_split_gate_up",
    )(w_gate_up)


def _experts_kernel(be_ref, nb_ref, x_ref, wg_ref, wl_ref, wd_ref, bg_ref, bl_ref, bd_ref, y_ref):
    i = pl.program_id(0)

    @pl.when(i < nb_ref[0])
    def _():
        x = x_ref[...].astype(BF16)
        gate = jnp.minimum(_dot(x, wg_ref[...]) + bg_ref[...], SWIGLU_LIMIT)
        lin = jnp.clip(_dot(x, wl_ref[...]) + bl_ref[...], -SWIGLU_LIMIT, SWIGLU_LIMIT)
        hmid = gate * _sigmoid(SWIGLU_ALPHA * gate) * (lin + 1.0)
        y_ref[...] = _dot(hmid.astype(BF16), wd_ref[...]) + bd_ref[...]

    @pl.when(i >= nb_ref[0])
    def _():
        y_ref[...] = jnp.zeros(y_ref.shape, y_ref.dtype)


def _experts(block_e, nb_used, xbuf, wg, wl, wd, bg, bl, bd):
    n_rows = xbuf.shape[0]
    d_ff = wg.shape[2]
    xmap = lambda i, be, nb: (jnp.minimum(i, nb[0] - 1), 0)
    wmap = lambda i, be, nb: (be[i], 0, 0)
    return pl.pallas_call(
        _experts_kernel,
        grid_spec=pltpu.PrefetchScalarGridSpec(
            num_scalar_prefetch=2,
            grid=(n_rows // EXPERT_ROWS,),
            in_specs=[pl.BlockSpec((EXPERT_ROWS, D_MODEL), xmap),
                      pl.BlockSpec((None, D_MODEL, d_ff), wmap), pl.BlockSpec((None, D_MODEL, d_ff), wmap),
                      pl.BlockSpec((None, d_ff, D_MODEL), wmap),
                      pl.BlockSpec((None, 1, d_ff), wmap), pl.BlockSpec((None, 1, d_ff), wmap),
                      pl.BlockSpec((None, 1, D_MODEL), wmap)],
            out_specs=pl.BlockSpec((EXPERT_ROWS, D_MODEL), lambda i, be, nb: (i, 0))),
        out_shape=jax.ShapeDtypeStruct(xbuf.shape, F32),
        compiler_params=pltpu.CompilerParams(dimension_semantics=("arbitrary",), vmem_limit_bytes=VMEM_LIMIT),
        name="moe_experts",
    )(block_e, nb_used, xbuf, wg, wl, wd, bg, bl, bd)


def _combine_kernel(dest_ref, h_ref, tg_ref, y_ref, o_ref, buf, sem, *, tb):
    def row_copy(t, k):
        return pltpu.make_async_copy(y_ref.at[pl.ds(dest_ref[t * TOP_K + k], 1), :], buf.at[k, pl.ds(t, 1), :], sem)

    _for_each_assignment(tb, lambda t, k: row_copy(t, k).start())
    _for_each_assignment(tb, lambda t, k: row_copy(t, k).wait())
    tg = tg_ref[...]
    acc = h_ref[...]
    for k in range(TOP_K):
        acc = acc + tg[:, k:k + 1] * buf[k]
    o_ref[...] = acc


def _combine(dest_flat, h, tg, ybuf, tb):
    t = h.shape[0]
    row = lambda i: (i, 0)
    return pl.pallas_call(
        functools.partial(_combine_kernel, tb=tb),
        grid=(t // tb,),
        in_specs=[pl.BlockSpec((tb * TOP_K,), lambda i: (i,), memory_space=pltpu.SMEM),
                  pl.BlockSpec((tb, D_MODEL), row), pl.BlockSpec((tb, LANES), row),
                  pl.BlockSpec(memory_space=pl.ANY)],
        out_specs=pl.BlockSpec((tb, D_MODEL), row),
        out_shape=jax.ShapeDtypeStruct((t, D_MODEL), F32),
        scratch_shapes=[pltpu.VMEM((TOP_K, tb, D_MODEL), F32), pltpu.SemaphoreType.DMA(())],
        compiler_params=pltpu.CompilerParams(dimension_semantics=("arbitrary",), vmem_limit_bytes=VMEM_LIMIT),
        name="moe_combine",
    )(dest_flat, h, tg, ybuf)


def _routing(flat_e, n_blocks):
    onehot = (flat_e[:, None] == jnp.arange(N_EXPERTS, dtype=jnp.int32)[None, :]).astype(jnp.int32)
    csum = jnp.cumsum(onehot, axis=0)
    rank = jnp.sum(onehot * csum, axis=1) - 1
    counts = csum[-1]
    padded = (counts + EXPERT_ROWS - 1) // EXPERT_ROWS * EXPERT_ROWS
    pad_end = jnp.cumsum(padded)
    pad_start = pad_end - padded
    dest = (pad_start[flat_e] + rank).astype(jnp.int32)
    block_start = jnp.arange(n_blocks, dtype=jnp.int32) * EXPERT_ROWS
    block_e = jnp.minimum(jnp.sum((pad_end[None, :] <= block_start[:, None]).astype(jnp.int32), axis=1),
                          N_EXPERTS - 1).astype(jnp.int32)
    nb_used = (pad_end[-1:] // EXPERT_ROWS).astype(jnp.int32)
    return dest, block_e, nb_used


def kernel(x_prompt, x_sample, cache_k, cache_v, page_table, state_gdn, state_conv, norm1_g, w_in, q_norm_g,
           k_norm_g, conv_w, a_log, dt_bias, gdn_norm_g, w_out, norm2_g, w_router, b_router, w_gate_up,
           b_gate_up, w_down, b_down):
    assert w_in.shape[0] == 1, "single layer"
    b, s, d = x_prompt.shape
    db, dl, _ = x_sample.shape
    tp, ts = b * s, db * dl

    w_in_p = jnp.pad(w_in[0], ((0, 0), (0, IN_COLS_PAD - IN_COLS))).astype(BF16)
    qg = jnp.tile(q_norm_g[0], N_HEADS)[None, :]
    kg = jnp.tile(k_norm_g[0], N_HEADS)[None, :]
    neg_a = jnp.pad(-jnp.exp(a_log[0]), (0, LANES - N_HEADS))[None, :]
    dtb = jnp.pad(dt_bias[0], (0, LANES - N_HEADS))[None, :]
    ng = jnp.tile(gdn_norm_g[0], 2)[None, :]
    wo = w_out[0].astype(BF16)
    wr = jnp.pad(w_router[0], ((0, 0), (0, LANES - N_EXPERTS)))
    br = jnp.pad(b_router[0], (0, LANES - N_EXPERTS), constant_values=NEG)[None, :]
    wg, wl = _split_gate_up(w_gate_up[0])
    bg = b_gate_up[0][:, None, 0::2]
    bl = b_gate_up[0][:, None, 1::2]
    wd = w_down[0].astype(BF16)
    bd = b_down[0][:, None, :]
    slopes = (2.0 ** (-8.0 * jnp.arange(1, N_HEADS + 1, dtype=F32) / N_HEADS)).astype(F32)

    xp2 = x_prompt.reshape(tp, d)
    xs2 = x_sample.reshape(ts, d)
    g1 = norm1_g[0][None, :]
    g2 = norm2_g[0][None, :]

    qp, kp, vp, gqp, zp, gbp = _inproj(xp2, g1, w_in_p, qg, kg, neg_a, dtb, tm=512)
    att_p = _moba_prompt(slopes, qp, kp, vp, b, s)
    gdn_p, gstate_p = _gdn(gqp.reshape(b, s, GDN_QKV), zp.reshape(b, s, WIDTH), gbp.reshape(b, s, LANES),
                           conv_w[0], jnp.zeros((b, CONV_W - 1, GDN_QKV), F32),
                           jnp.zeros((b, N_HEADS, HEAD_DIM, HEAD_DIM), F32), ng, b, s, s)

    qs, ks, vs, gqs, zs, gbs = _inproj(xs2, g1, w_in_p, qg, kg, neg_a, dtb, tm=ts)
    page_major = lambda c: jnp.transpose(c[0], (0, 2, 3, 1)).reshape(c.shape[1], WIDTH, PAGE)
    att_s = _moba_sample(page_table, qs, ks, vs, page_major(cache_k), page_major(cache_v), db, dl)
    padl = lambda a: jnp.pad(a.reshape(db, dl, -1), ((0, 0), (0, GDN_CHUNK - dl), (0, 0)))
    gdn_s, gstate_s = _gdn(padl(gqs), padl(zs), padl(gbs), conv_w[0], state_conv[0], state_gdn[0], ng,
                           db, GDN_CHUNK, dl)
    gdn_s = gdn_s[:, :dl].reshape(ts, WIDTH)

    hp, mp, tip, tgp = _outproj(xp2, att_p, gdn_p.reshape(tp, WIDTH), wo, g2, wr, br, tm=512)
    hs, msm, tis, tgs = _outproj(xs2, att_s, gdn_s, wo, g2, wr, br, tm=ts)

    n_assign = (tp + ts) * TOP_K
    n_blocks = (n_assign + N_EXPERTS * (EXPERT_ROWS - 1)) // EXPERT_ROWS + 1
    flat_e = jnp.concatenate([tip[:, :TOP_K].reshape(-1), tis[:, :TOP_K].reshape(-1)])
    dest, block_e, nb_used = _routing(flat_e, n_blocks)
    dest_p, dest_s = dest[:tp * TOP_K], dest[tp * TOP_K:]
    xbuf = jnp.zeros((n_blocks * EXPERT_ROWS, D_MODEL), F32)
    xbuf = _dispatch(dest_p, mp, xbuf, tb=256)
    xbuf = _dispatch(dest_s, msm, xbuf, tb=ts)
    ybuf = _experts(block_e, nb_used, xbuf, wg, wl, wd, bg, bl, bd)
    y_prompt = _combine(dest_p, hp, tgp, ybuf, tb=256).reshape(b, s, d)
    y_sample = _combine(dest_s, hs, tgs, ybuf, tb=ts).reshape(db, dl, d)

    k_prompt = kp.reshape(1, b, s // PAGE, PAGE, N_HEADS, HEAD_DIM)
    v_prompt = vp.reshape(1, b, s // PAGE, PAGE, N_HEADS, HEAD_DIM)
    k_sample = ks.reshape(1, db, dl, N_HEADS, HEAD_DIM)
    v_sample = vs.reshape(1, db, dl, N_HEADS, HEAD_DIM)
    conv_prompt = gqp.reshape(b, s, GDN_QKV)[:, s - (CONV_W - 1):][None]
    conv_sample = jnp.concatenate([state_conv[0], gqs.reshape(db, dl, GDN_QKV)], axis=1)[:, -(CONV_W - 1):][None]
    return (y_prompt, y_sample, k_prompt, v_prompt, k_sample, v_sample,
            gstate_p[None], conv_prompt, gstate_s[None], conv_sample)
```

```python
import functools

import jax
import jax.numpy as jnp
import numpy as np
from jax import lax
from jax.experimental import pallas as pl
from jax.experimental.pallas import tpu as pltpu

F32 = jnp.float32
BF16 = jnp.bfloat16

LANES = 128
SUBLANES = 8
HEAD_DIM = 64
N_HEADS = 8
PAIR = 2 * HEAD_DIM
N_PAIRS = N_HEADS // 2
WIDTH = N_HEADS * HEAD_DIM
D_MODEL = 1024
GDN_QKV = 3 * WIDTH
IN_COLS = 3 * WIDTH + GDN_QKV + WIDTH + 2 * N_HEADS
IN_COLS_PAD = 3712
OFF_GQ = 3 * WIDTH
OFF_Z = OFF_GQ + GDN_QKV
OFF_AB = OFF_Z + WIDTH
MOBA_BLOCK = 256
MOBA_TOPK = 3
Q_CHUNK = 128
PAGE = 128
CONV_W = 4
GDN_CHUNK = 64
N_EXPERTS = 32
TOP_K = 4
SWIGLU_LIMIT = 7.0
SWIGLU_ALPHA = 1.702
EXPERT_ROWS = 256
SAMPLE_PAGES_PER_STEP = 16
SAMPLE_UNROLL = 4
GDN_GROUP = 8
GDN_UNROLL = 4
EPS = 1e-6
NEG = -1e30
VMEM_LIMIT = 56 * 1024 * 1024

NN = (((1,), (0,)), ((), ()))
NT = (((1,), (1,)), ((), ()))


def _dot(a, b, dims=NN):
    return lax.dot_general(a, b, dims, preferred_element_type=F32)


def _split(x):
    hi = x.astype(BF16)
    lo = (x - hi.astype(F32)).astype(BF16)
    return hi, lo


def _dot3(a, b, dims=NN):
    ah, al = _split(a)
    bh, bl = _split(b)
    return _dot(ah, bh, dims) + (_dot(ah, bl, dims) + _dot(al, bh, dims))


def _dot_exact_rhs(a, b_bf16, terms=2):
    out = None
    r = a
    for _ in range(terms):
        h = r.astype(BF16)
        part = _dot(h, b_bf16)
        out = part if out is None else out + part
        r = r - h.astype(F32)
    return out


def _dot_exact_lhs(a_bf16, b, terms=2):
    out = None
    r = b
    for _ in range(terms):
        h = r.astype(BF16)
        part = _dot(a_bf16, h)
        out = part if out is None else out + part
        r = r - h.astype(F32)
    return out


def _group_matrix(n, group, value):
    r = lax.broadcasted_iota(jnp.int32, (n, n), 0) // group
    c = lax.broadcasted_iota(jnp.int32, (n, n), 1) // group
    return jnp.where(r == c, value, 0.0).astype(BF16)


def _group_mean_matrix(n, group):
    return _group_matrix(n, group, 1.0 / group)


def _sigmoid(x):
    return 1.0 / (1.0 + jnp.exp(-x))


def _silu(x):
    return x * _sigmoid(x)


def _inproj_kernel(x_ref, g1_ref, w_ref, qg_ref, kg_ref, neg_a_ref, dtb_ref,
                   q_ref, k_ref, v_ref, gq_ref, z_ref, gb_ref):
    x = x_ref[...]
    ms = jnp.mean(x * x, axis=-1, keepdims=True)
    n = x * lax.rsqrt(ms + EPS) * g1_ref[...]
    p = _dot(n.astype(BF16), w_ref[...])
    gm = _group_mean_matrix(2 * LANES, HEAD_DIM)

    def head_norm(t, gain):
        sq = t * t
        parts = [_dot_exact_rhs(sq[:, c:c + 2 * LANES], gm) for c in range(0, WIDTH, 2 * LANES)]
        return t * lax.rsqrt(jnp.concatenate(parts, axis=1) + EPS) * gain

    q_ref[...] = head_norm(p[:, 0:WIDTH], qg_ref[...])
    k_ref[...] = head_norm(p[:, WIDTH:2 * WIDTH], kg_ref[...])
    v_ref[...] = p[:, 2 * WIDTH:3 * WIDTH]
    gq_ref[...] = p[:, OFF_GQ:OFF_Z]
    z_ref[...] = p[:, OFF_Z:OFF_AB]
    ab = p[:, OFF_AB:IN_COLS_PAD]
    t = ab + dtb_ref[...]
    softplus = jnp.maximum(t, 0.0) + jnp.log(1.0 + jnp.exp(-jnp.abs(t)))
    lane = lax.broadcasted_iota(jnp.int32, ab.shape, 1)
    gb_ref[...] = jnp.where(lane < N_HEADS, neg_a_ref[...] * softplus, _sigmoid(ab))


def _inproj(x2, g1, w_pad, qg, kg, neg_a, dtb, tm):
    t = x2.shape[0]
    row = lambda i: (i, 0)
    fixed = lambda i: (0, 0)
    outs = [jax.ShapeDtypeStruct((t, WIDTH), F32)] * 3 + [
        jax.ShapeDtypeStruct((t, GDN_QKV), F32), jax.ShapeDtypeStruct((t, WIDTH), F32),
        jax.ShapeDtypeStruct((t, LANES), F32)]
    return pl.pallas_call(
        _inproj_kernel,
        grid=(t // tm,),
        in_specs=[pl.BlockSpec((tm, D_MODEL), row), pl.BlockSpec((1, D_MODEL), fixed),
                  pl.BlockSpec((D_MODEL, IN_COLS_PAD), fixed), pl.BlockSpec((1, WIDTH), fixed),
                  pl.BlockSpec((1, WIDTH), fixed), pl.BlockSpec((1, LANES), fixed),
                  pl.BlockSpec((1, LANES), fixed)],
        out_specs=[pl.BlockSpec((tm, WIDTH), row)] * 3 + [
            pl.BlockSpec((tm, GDN_QKV), row), pl.BlockSpec((tm, WIDTH), row), pl.BlockSpec((tm, LANES), row)],
        out_shape=outs,
        compiler_params=pltpu.CompilerParams(dimension_semantics=("arbitrary",), vmem_limit_bytes=VMEM_LIMIT),
        name="inproj",
    )(x2, g1, w_pad, qg, kg, neg_a, dtb)


def _moba_prompt_kernel(slopes_ref, q_ref, k_ref, v_ref, o_ref, kmean_ref, vt_ref, acc_ref, *, n_blocks):
    hp = pl.program_id(1)
    ob = pl.program_id(2)

    @pl.when(ob == 0)
    def _():
        for n in range(n_blocks):
            blk = slice(n * MOBA_BLOCK, (n + 1) * MOBA_BLOCK)
            kmean_ref[n:n + 1, :] = jnp.mean(k_ref[blk, :], axis=0, keepdims=True)
            vt_ref[:, blk] = v_ref[blk, :].T.astype(BF16)

    bq = MOBA_BLOCK
    cols = 2 * bq
    q = q_ref[...]
    lane = lax.broadcasted_iota(jnp.int32, (bq, PAIR), 1)
    qst = jnp.concatenate([jnp.where(lane < HEAD_DIM, q, 0.0), jnp.where(lane < HEAD_DIM, 0.0, q)], axis=0)
    qst_b = (qst * HEAD_DIM ** -0.5).astype(BF16)
    bidx = lax.broadcasted_iota(jnp.int32, (n_blocks, cols), 0)
    gate = jnp.where(bidx < ob, _dot3(kmean_ref[...], qst, NT), -jnp.inf)
    valid = jnp.where(bidx < ob, 1.0, 0.0)
    sel = jnp.zeros((n_blocks, cols), F32)
    for _ in range(min(MOBA_TOPK, n_blocks - 1)):
        mx = jnp.max(gate, axis=0, keepdims=True)
        idx = jnp.min(jnp.where(gate == mx, bidx, n_blocks), axis=0, keepdims=True)
        hit = bidx == idx
        sel = jnp.where(hit, valid, sel)
        gate = jnp.where(hit, -jnp.inf, gate)

    col = lax.broadcasted_iota(jnp.int32, (1, cols), 1)
    slope = jnp.where(col < bq, slopes_ref[2 * hp], slopes_ref[2 * hp + 1])
    rel = (lax.broadcasted_iota(jnp.int32, (bq, cols), 1) % bq
           - lax.broadcasted_iota(jnp.int32, (bq, cols), 0)).astype(F32)
    alibi = slope * rel

    def logits(n):
        start = pl.multiple_of(n * MOBA_BLOCK, MOBA_BLOCK)
        kb = k_ref[pl.ds(start, MOBA_BLOCK), :].astype(BF16)
        return _dot(kb, qst_b, NT) - alibi, start

    s, start = logits(ob)
    s = jnp.where(rel >= 0, s, NEG)
    m0 = jnp.max(s, axis=0, keepdims=True)
    p = jnp.exp(s - m0)
    l0 = jnp.sum(p, axis=0, keepdims=True)
    acc_ref[...] = _dot(vt_ref[:, pl.ds(start, MOBA_BLOCK)], p.astype(BF16))

    def body(pair, carry):
        m_i, l_i = carry
        n = 2 * pair
        start = pl.multiple_of(n * MOBA_BLOCK, 2 * MOBA_BLOCK)
        raw = _dot(k_ref[pl.ds(start, 2 * MOBA_BLOCK), :].astype(BF16), qst_b, NT)
        far = slope * jnp.full((1, cols), (ob - n) * MOBA_BLOCK, jnp.int32).astype(F32)
        halves = []
        for i in range(2):
            sel_i = jnp.sum(jnp.where(bidx == n + i, sel, 0.0), axis=0, keepdims=True)
            s = raw[i * MOBA_BLOCK:(i + 1) * MOBA_BLOCK] - alibi - (far - slope * float(i * MOBA_BLOCK))
            halves.append(jnp.where(sel_i > 0, s, NEG))
        m_new = jnp.maximum(m_i, jnp.maximum(jnp.max(halves[0], axis=0, keepdims=True),
                                             jnp.max(halves[1], axis=0, keepdims=True)))
        alpha = jnp.exp(m_i - m_new)
        p = [jnp.exp(s - m_new) for s in halves]
        pv = _dot(vt_ref[:, pl.ds(start, 2 * MOBA_BLOCK)], jnp.concatenate(p, axis=0).astype(BF16))
        acc_ref[...] = acc_ref[...] * alpha + pv
        return m_new, l_i * alpha + (jnp.sum(p[0], axis=0, keepdims=True) + jnp.sum(p[1], axis=0, keepdims=True))

    _, l_f = lax.fori_loop(0, (ob + 1) // 2, body, (m0, l0))
    o_t = acc_ref[...] * (1.0 / l_f)
    o_ref[...] = jnp.concatenate([o_t[:HEAD_DIM, :bq], o_t[HEAD_DIM:, bq:]], axis=0).T


def _moba_prompt(slopes, q, k, v, batch, seq):
    n_blocks = seq // MOBA_BLOCK
    qmap = lambda b, hp, ob: (b * n_blocks + ob, hp)
    kmap = lambda b, hp, ob: (b, hp)
    return pl.pallas_call(
        functools.partial(_moba_prompt_kernel, n_blocks=n_blocks),
        grid=(batch, N_PAIRS, n_blocks),
        in_specs=[pl.BlockSpec(memory_space=pltpu.SMEM),
                  pl.BlockSpec((MOBA_BLOCK, PAIR), qmap), pl.BlockSpec((seq, PAIR), kmap),
                  pl.BlockSpec((seq, PAIR), kmap)],
        out_specs=pl.BlockSpec((MOBA_BLOCK, PAIR), qmap),
        out_shape=jax.ShapeDtypeStruct((batch * seq, WIDTH), F32),
        scratch_shapes=[pltpu.VMEM((n_blocks, PAIR), F32), pltpu.VMEM((PAIR, seq), BF16),
                        pltpu.VMEM((PAIR, 2 * MOBA_BLOCK), F32)],
        compiler_params=pltpu.CompilerParams(dimension_semantics=("arbitrary",) * 3, vmem_limit_bytes=VMEM_LIMIT),
        name="moba_prompt",
    )(slopes, q, k, v)


def _moba_sample_kernel(pt_ref, q_ref, kn_ref, vn_ref, *refs, n_blk, n_new, past, pps):
    k_refs, v_refs = refs[:pps], refs[pps:2 * pps]
    o_ref, q_scr, s_scr, l_scr, acc_scr, gate_scr = refs[2 * pps:]
    ph = pl.program_id(1)
    j = pl.program_id(2)
    rows = N_HEADS * n_new
    scale = HEAD_DIM ** -0.5
    n_steps = (past // PAGE) // pps
    head_of_lane = lax.broadcasted_iota(jnp.int32, (n_new, WIDTH), 1) // HEAD_DIM

    @pl.when((ph == 0) & (j == 0))
    def _():
        q = q_ref[...]
        for h in range(N_HEADS):
            q_scr[h * n_new:(h + 1) * n_new, :] = jnp.where(head_of_lane == h, q, 0.0)

        gate_scr[...] = jnp.full((rows, LANES), -jnp.inf, F32)

    ppb = MOBA_BLOCK // PAGE
    lane = lax.broadcasted_iota(jnp.int32, (rows, LANES), 1)

    @pl.when(ph == 0)
    def _():
        qb = q_scr[...].astype(BF16)
        gate = gate_scr[...]
        for blk in range(pps // ppb):
            total = None
            for pg in range(blk * ppb, (blk + 1) * ppb):
                col = pl.multiple_of((j * pps + pg) * PAGE, PAGE)
                s = _dot(qb, k_refs[pg][...].astype(BF16))
                s_scr[:, pl.ds(col, PAGE)] = s
                total = s if total is None else total + s
            mean = jnp.sum(total, axis=1, keepdims=True) * (1.0 / MOBA_BLOCK)
            gate = jnp.where(lane == j * (pps // ppb) + blk, mean, gate)
        gate_scr[...] = gate

    slope_col = jnp.concatenate(
        [jnp.full((n_new, 1), 2.0 ** (-8.0 * (h + 1) / N_HEADS), F32) for h in range(N_HEADS)], axis=0)
    qpos = (past + lax.broadcasted_iota(jnp.int32, (rows, 1), 0) % n_new).astype(F32)

    def block_logits(n):
        start = pl.multiple_of(n * MOBA_BLOCK, MOBA_BLOCK)
        raw = s_scr[:, pl.ds(start, MOBA_BLOCK)]
        kpos = (n * MOBA_BLOCK + lax.broadcasted_iota(jnp.int32, (rows, MOBA_BLOCK), 1)).astype(F32)
        return raw * scale - slope_col * (qpos - kpos), start

    @pl.when((ph == 1) & (j == 0))
    def _():
        gate = gate_scr[...]
        sel = jnp.zeros((rows, LANES), F32)
        for _ in range(min(MOBA_TOPK, n_blk)):
            mx = jnp.max(gate, axis=1, keepdims=True)
            idx = jnp.min(jnp.where(gate == mx, lane, LANES), axis=1, keepdims=True)
            hit = lane == idx
            sel = jnp.where(hit, 1.0, sel)
            gate = jnp.where(hit, -jnp.inf, gate)

        own = _dot(q_scr[...], kn_ref[...], NT)
        li = lax.broadcasted_iota(jnp.int32, (rows, n_new), 0) % n_new
        ri = lax.broadcasted_iota(jnp.int32, (rows, n_new), 1)
        d_own = (li - ri).astype(F32)
        lg_own = jnp.where(d_own >= 0, own * scale - slope_col * d_own, NEG)
        m0 = jnp.max(lg_own, axis=1, keepdims=True)

        def masked_logits(n):
            lg, start = block_logits(n)
            picked = jnp.sum(jnp.where(lane == n, sel, 0.0), axis=1, keepdims=True) > 0
            return jnp.where(picked, lg, NEG), start

        def max_body(n, mv):
            lg, _ = masked_logits(n)
            return jnp.maximum(mv, jnp.maximum(lg[:, :LANES], lg[:, LANES:]))

        mv = lax.fori_loop(0, n_blk, max_body, jnp.full((rows, LANES), NEG, F32), unroll=SAMPLE_UNROLL)
        m = jnp.maximum(m0, jnp.max(mv, axis=1, keepdims=True))
        p_own = jnp.exp(lg_own - m)

        def p_body(n, lv):
            lg, start = masked_logits(n)
            p = jnp.exp(lg - m)
            s_scr[:, pl.ds(start, MOBA_BLOCK)] = p
            return lv + (p[:, :LANES] + p[:, LANES:])

        lv = lax.fori_loop(0, n_blk, p_body, jnp.zeros((rows, LANES), F32), unroll=SAMPLE_UNROLL)
        l = jnp.sum(p_own, axis=1, keepdims=True) + jnp.sum(lv, axis=1, keepdims=True)
        l_scr[...] = jnp.broadcast_to(l, (rows, LANES))
        acc_scr[...] = _dot(p_own, vn_ref[...])

    @pl.when(ph == 1)
    def _():
        acc = acc_scr[...]
        for pg in range(pps):
            col = pl.multiple_of((j * pps + pg) * PAGE, PAGE)
            acc = acc + _dot(s_scr[:, pl.ds(col, PAGE)].astype(BF16), v_refs[pg][...].astype(BF16), NT)
        acc_scr[...] = acc

    @pl.when((ph == 1) & (j == n_steps - 1))
    def _():
        out = jnp.zeros((n_new, WIDTH), F32)
        for h in range(N_HEADS):
            rs = slice(h * n_new, (h + 1) * n_new)
            out = out + jnp.where(head_of_lane == h, acc_scr[rs, :] * (1.0 / l_scr[rs, 0:1]), 0.0)
        o_ref[...] = out


def _moba_sample(page_table, q, k_new, v_new, cache_kt, cache_vt, n_seq, n_new):
    n_pages = page_table.shape[1]
    past = n_pages * PAGE
    n_blk = past // MOBA_BLOCK
    pps = min(SAMPLE_PAGES_PER_STEP, n_pages)
    n_steps = n_pages // pps
    assert n_blk * MOBA_BLOCK == past and n_steps * pps == n_pages and n_new <= SUBLANES
    tok = lambda b, ph, j, pt: (b, 0)

    def kpage(pg):
        return lambda b, ph, j, pt: (pt[b, pps * jnp.where(ph == 0, j, n_steps - 1) + pg], 0, 0)

    def vpage(pg):
        return lambda b, ph, j, pt: (pt[b, pps * jnp.where(ph == 0, 0, j) + pg], 0, 0)

    page_block = (None, WIDTH, PAGE)
    rows = N_HEADS * n_new
    return pl.pallas_call(
        functools.partial(_moba_sample_kernel, n_blk=n_blk, n_new=n_new, past=past, pps=pps),
        grid_spec=pltpu.PrefetchScalarGridSpec(
            num_scalar_prefetch=1,
            grid=(n_seq, 2, n_steps),
            in_specs=[pl.BlockSpec((n_new, WIDTH), tok)] * 3
            + [pl.BlockSpec(page_block, kpage(pg)) for pg in range(pps)]
            + [pl.BlockSpec(page_block, vpage(pg)) for pg in range(pps)],
            out_specs=pl.BlockSpec((n_new, WIDTH), tok),
            scratch_shapes=[pltpu.VMEM((rows, WIDTH), F32),
                            pltpu.VMEM((rows, past), F32),
                            pltpu.VMEM((rows, LANES), F32),
                            pltpu.VMEM((rows, WIDTH), F32),
                            pltpu.VMEM((rows, LANES), F32)]),
        out_shape=jax.ShapeDtypeStruct((n_seq * n_new, WIDTH), F32),
        compiler_params=pltpu.CompilerParams(dimension_semantics=("arbitrary",) * 3, vmem_limit_bytes=VMEM_LIMIT),
        name="moba_sample",
    )(page_table, q, k_new, v_new, *([cache_kt] * pps), *([cache_vt] * pps))


def _gdn_kernel(xq_ref, xk_ref, xv_ref, z_ref, gb_ref, cwq_ref, cwk_ref, cwv_ref, cbq_ref, cbk_ref, cbv_ref,
                s0_ref, ng_ref, o_ref, s_out_ref, xs_scr, act_scr, gate_scr, pk_scr, q_scr, n_scr, dec_scr,
                *, seq, valid, row_block):
    hp = pl.program_id(1)
    c = GDN_CHUNK
    c2 = 2 * c
    pad = SUBLANES
    lane_r = lax.broadcasted_iota(jnp.int32, (row_block, PAIR), 1)
    half_mean = _group_mean_matrix(PAIR, HEAD_DIM)
    half_sum = _group_matrix(PAIR, HEAD_DIM, 1.0)

    er = lax.broadcasted_iota(jnp.int32, (LANES, LANES), 0)
    expand = [jnp.where(er == off + 2 * hp + a, 1.0, 0.0).astype(BF16) for off in (0, N_HEADS) for a in range(2)]

    for i, (x_ref, cb_ref) in enumerate(((xq_ref, cbq_ref), (xk_ref, cbk_ref), (xv_ref, cbv_ref))):
        xs_scr[i, pad - (CONV_W - 1):pad, :] = cb_ref[...]
        xs_scr[i, pad:pad + seq, :] = x_ref[...]
    for r0 in range(0, seq, row_block):
        live = (lax.broadcasted_iota(jnp.int32, (row_block, PAIR), 0) + r0) < valid
        for i, cw_ref in enumerate((cwq_ref, cwk_ref, cwv_ref)):
            conv = None
            for t in range(CONV_W):
                start = pad - (CONV_W - 1) + t + r0
                term = xs_scr[i, start:start + row_block, :] * cw_ref[t:t + 1, :]
                conv = term if conv is None else conv + term
            act = _silu(conv)
            if i < 2:
                ss = _dot_exact_rhs(act * act, half_sum)
                act = act * lax.rsqrt(ss + EPS)
            act_scr[i, r0:r0 + row_block, :] = jnp.where(live, act, 0.0)
        gb = gb_ref[r0:r0 + row_block, :]
        for e in range(4):
            gate_scr[e, r0:r0 + row_block, :] = jnp.where(live, _dot_exact_rhs(gb, expand[e], terms=3), 0.0)

    ri = lax.broadcasted_iota(jnp.int32, (c2, c2), 0)
    ci = lax.broadcasted_iota(jnp.int32, (c2, c2), 1)
    delta = jnp.where((ri // c) == (ci // c), ri - ci, -1)
    causal = delta >= 0
    strict = delta > 0
    tril = jnp.where(causal, 1.0, 0.0).astype(BF16)
    eye = jnp.where(ri == ci, 1.0, 0.0)
    lane_c = lax.broadcasted_iota(jnp.int32, (c, PAIR), 1)
    first = lane_c < HEAD_DIM
    diag_blocks = (lax.broadcasted_iota(jnp.int32, (PAIR, PAIR), 0) // HEAD_DIM) == (
        lax.broadcasted_iota(jnp.int32, (PAIR, PAIR), 1) // HEAD_DIM)
    scale = HEAD_DIM ** -0.5

    def stack(x):
        return jnp.concatenate([jnp.where(first, x, 0.0), jnp.where(first, 0.0, x)], axis=0)

    def unstack(x):
        return x[:c] + x[c:]

    zeros = jnp.zeros((HEAD_DIM, HEAD_DIM), F32)
    s_init = jnp.concatenate([jnp.concatenate([s0_ref[0], zeros], axis=1),
                              jnp.concatenate([zeros, s0_ref[1]], axis=1)], axis=0)

    def each(fn, *lists):
        return [fn(*args) for args in zip(*lists)]

    def prepare(ics):
        r0s = [pl.multiple_of(ic * c, c) for ic in ics]
        qc = [act_scr[0, pl.ds(r0, c), :] * scale for r0 in r0s]
        kc = [act_scr[1, pl.ds(r0, c), :] for r0 in r0s]
        vc = [act_scr[2, pl.ds(r0, c), :] for r0 in r0s]
        g_st = [jnp.concatenate([gate_scr[0, pl.ds(r0, c), :], gate_scr[1, pl.ds(r0, c), :]], axis=0) for r0 in r0s]
        b_st = [jnp.concatenate([gate_scr[2, pl.ds(r0, c), :], gate_scr[3, pl.ds(r0, c), :]], axis=0) for r0 in r0s]
        gc_st = each(lambda g: _dot_exact_lhs(tril, g, terms=3), g_st)
        decay = each(lambda g: jnp.where(causal, jnp.exp(jnp.minimum(g - g.T, 0.0)), 0.0), gc_st)
        k_st = each(stack, kc)
        kb_st = each(lambda k, b: k * b, k_st, b_st)
        k_stb = each(lambda k: k.astype(BF16), k_st)
        a_mat = each(lambda kb, k, d: jnp.where(strict, _dot(kb.astype(BF16), k, NT) * d, 0.0), kb_st, k_stb, decay)
        qk = each(lambda q, k, d: jnp.where(causal, _dot(stack(q).astype(BF16), k, NT) * d, 0.0).astype(BF16),
                  qc, k_stb, decay)
        base = each(lambda a: jnp.where(ri // SUBLANES == ci // SUBLANES, a, 0.0), a_mat)
        b2 = each(lambda b: _dot3(b, b), base)
        t_mat = each(lambda b: eye - b, base)
        t_mat = each(lambda t, x: t + _dot3(t, x), t_mat, b2)
        b4 = each(lambda x: _dot3(x, x), b2)
        t_mat = each(lambda t, x: t + _dot3(t, x), t_mat, b4)
        size = SUBLANES
        while size < c:
            off = each(lambda a: jnp.where(((ri // size) ^ (ci // size)) == 1, a, 0.0), a_mat)
            t_b = each(lambda t: t.astype(BF16), t_mat)
            tm = each(lambda t, x: _dot(t, x.astype(BF16)), t_b, off)
            t_mat = each(lambda t, tb_, x: t - _dot(x.astype(BF16), tb_), t_mat, t_b, tm)
            size *= 2
        rhs = each(lambda v, b, kb, g: jnp.concatenate([stack(v) * b, kb * jnp.exp(g)], axis=1).astype(BF16),
                   vc, b_st, kb_st, gc_st)
        uw = each(lambda t, r: _dot(t.astype(BF16), r), t_mat, rhs)
        u2 = each(lambda x: unstack(x[:, :PAIR]), uw)
        w2 = each(lambda x: unstack(x[:, PAIR:]), uw)
        gc2 = each(lambda g: jnp.where(first, g[:c], g[c:]), gc_st)
        g_last = each(lambda g: g[c - 1:c, :], gc2)
        kdt = each(lambda k, gl, g: (k * jnp.exp(gl - g)).T.astype(BF16), kc, g_last, gc2)
        kn = each(lambda kt, w, u: _dot(kt, jnp.concatenate([w, u], axis=1).astype(BF16)), kdt, w2, u2)
        qwu = each(lambda m, w, u: _dot(m, jnp.concatenate([stack(w), stack(u)], axis=1).astype(BF16)), qk, w2, u2)
        for i, ic in enumerate(ics):
            p_mat = qc[i] * jnp.exp(gc2[i]) - unstack(qwu[i][:, :PAIR])
            k_mat = jnp.where(diag_blocks, kn[i][:, :PAIR], 0.0)
            pk_scr[ic] = jnp.concatenate([p_mat, k_mat], axis=0).astype(BF16)
            q_scr[ic] = unstack(qwu[i][:, PAIR:])
            n_scr[ic] = jnp.where(diag_blocks, kn[i][:, PAIR:], 0.0)
            dec_scr[ic] = jnp.broadcast_to(jnp.exp(g_last[i]), (SUBLANES, PAIR))

    n_chunks = seq // c
    group = GDN_GROUP if n_chunks % GDN_GROUP == 0 else 1

    def prepare_group(i, carry):
        prepare([group * i + j for j in range(group)])
        return carry

    lax.fori_loop(0, n_chunks // group, prepare_group, 0)

    def recur(ic, s2):
        r0 = pl.multiple_of(ic * c, c)
        ps = _dot(pk_scr[ic], s2.astype(BF16))
        o2 = ps[:c] + q_scr[ic]
        s2 = s2 * dec_scr[ic, 0:1, :] - ps[c:] + n_scr[ic]
        ms = _dot_exact_rhs(o2 * o2, half_mean)
        zc = z_ref[pl.ds(r0, c), :]
        o_ref[pl.ds(r0, c), :] = o2 * lax.rsqrt(ms + EPS) * ng_ref[...] * _silu(zc)
        return s2

    s_fin = lax.fori_loop(0, n_chunks, recur, s_init, unroll=GDN_UNROLL if n_chunks % GDN_UNROLL == 0 else 1)
    s_out_ref[0] = s_fin[:HEAD_DIM, :HEAD_DIM]
    s_out_ref[1] = s_fin[HEAD_DIM:, HEAD_DIM:]


def _gdn(gq, z, gb, conv_w, conv_buf, s0, ng, n_seq, seq, valid):
    row_block = min(seq, 256)
    n_chunks = seq // GDN_CHUNK
    seq_map = lambda off: (lambda b, hp: (b, 0, off + hp))
    cw_map = lambda off: (lambda b, hp: (0, off + hp))
    kern = functools.partial(_gdn_kernel, seq=seq, valid=valid, row_block=row_block)
    sb = pl.BlockSpec((None, seq, PAIR), seq_map(0))
    return pl.pallas_call(
        kern,
        grid=(n_seq, N_PAIRS),
        in_specs=[pl.BlockSpec((None, seq, PAIR), seq_map(0)), pl.BlockSpec((None, seq, PAIR), seq_map(N_PAIRS)),
                  pl.BlockSpec((None, seq, PAIR), seq_map(2 * N_PAIRS)), sb,
                  pl.BlockSpec((None, seq, LANES), lambda b, hp: (b, 0, 0)),
                  pl.BlockSpec((CONV_W, PAIR), cw_map(0)), pl.BlockSpec((CONV_W, PAIR), cw_map(N_PAIRS)),
                  pl.BlockSpec((CONV_W, PAIR), cw_map(2 * N_PAIRS)),
                  pl.BlockSpec((None, CONV_W - 1, PAIR), seq_map(0)),
                  pl.BlockSpec((None, CONV_W - 1, PAIR), seq_map(N_PAIRS)),
                  pl.BlockSpec((None, CONV_W - 1, PAIR), seq_map(2 * N_PAIRS)),
                  pl.BlockSpec((None, 2, HEAD_DIM, HEAD_DIM), lambda b, hp: (b, hp, 0, 0)),
                  pl.BlockSpec((1, PAIR), lambda b, hp: (0, 0))],
        out_specs=[sb, pl.BlockSpec((None, 2, HEAD_DIM, HEAD_DIM), lambda b, hp: (b, hp, 0, 0))],
        out_shape=[jax.ShapeDtypeStruct((n_seq, seq, WIDTH), F32),
                   jax.ShapeDtypeStruct((n_seq, N_HEADS, HEAD_DIM, HEAD_DIM), F32)],
        scratch_shapes=[pltpu.VMEM((3, seq + SUBLANES, PAIR), F32), pltpu.VMEM((3, seq, PAIR), F32),
                        pltpu.VMEM((4, seq, PAIR), F32),
                        pltpu.VMEM((n_chunks, GDN_CHUNK + PAIR, PAIR), BF16),
                        pltpu.VMEM((n_chunks, GDN_CHUNK, PAIR), F32),
                        pltpu.VMEM((n_chunks, PAIR, PAIR), F32),
                        pltpu.VMEM((n_chunks, SUBLANES, PAIR), F32)],
        compiler_params=pltpu.CompilerParams(dimension_semantics=("arbitrary",) * 2, vmem_limit_bytes=VMEM_LIMIT),
        name="gdn",
    )(gq, gq, gq, z, gb, conv_w, conv_w, conv_w, conv_buf, conv_buf, conv_buf, s0, ng)


def _outproj_kernel(x_ref, att_ref, gdn_ref, wo_ref, g2_ref, wr_ref, br_ref, h_ref, m_ref, ti_ref, tg_ref):
    mix = jnp.concatenate([att_ref[...], gdn_ref[...]], axis=1).astype(BF16)
    h = x_ref[...] + _dot(mix, wo_ref[...])
    h_ref[...] = h
    ms = jnp.mean(h * h, axis=-1, keepdims=True)
    m = h * lax.rsqrt(ms + EPS) * g2_ref[...]
    m_ref[...] = m
    logits = _dot3(m, wr_ref[...]) + br_ref[...]
    lane = lax.broadcasted_iota(jnp.int32, logits.shape, 1)
    ti = jnp.zeros(logits.shape, jnp.int32)
    tv = jnp.full(logits.shape, NEG, F32)
    for r in range(TOP_K):
        mx = jnp.max(logits, axis=1, keepdims=True)
        idx = jnp.min(jnp.where(logits == mx, lane, LANES), axis=1, keepdims=True)
        ti = jnp.where(lane == r, idx, ti)
        tv = jnp.where(lane == r, mx, tv)
        logits = jnp.where(lane == idx, -jnp.inf, logits)
    e = jnp.exp(tv - jnp.max(tv, axis=1, keepdims=True))
    ti_ref[...] = ti
    tg_ref[...] = e / jnp.sum(e, axis=1, keepdims=True)


def _outproj(x2, att, gdn, wo, g2, wr_pad, br_pad, tm):
    t = x2.shape[0]
    row = lambda i: (i, 0)
    fixed = lambda i: (0, 0)
    return pl.pallas_call(
        _outproj_kernel,
        grid=(t // tm,),
        in_specs=[pl.BlockSpec((tm, D_MODEL), row), pl.BlockSpec((tm, WIDTH), row), pl.BlockSpec((tm, WIDTH), row),
                  pl.BlockSpec((D_MODEL, D_MODEL), fixed), pl.BlockSpec((1, D_MODEL), fixed),
                  pl.BlockSpec((D_MODEL, LANES), fixed), pl.BlockSpec((1, LANES), fixed)],
        out_specs=[pl.BlockSpec((tm, D_MODEL), row), pl.BlockSpec((tm, D_MODEL), row),
                   pl.BlockSpec((tm, LANES), row), pl.BlockSpec((tm, LANES), row)],
        out_shape=[jax.ShapeDtypeStruct((t, D_MODEL), F32), jax.ShapeDtypeStruct((t, D_MODEL), F32),
                   jax.ShapeDtypeStruct((t, LANES), jnp.int32), jax.ShapeDtypeStruct((t, LANES), F32)],
        compiler_params=pltpu.CompilerParams(dimension_semantics=("arbitrary",), vmem_limit_bytes=VMEM_LIMIT),
        name="outproj",
    )(x2, att, gdn, wo, g2, wr_pad, br_pad)


def _for_each_assignment(tb, fn):
    def body(g, carry):
        base = pl.multiple_of(g * SUBLANES, SUBLANES)
        for j in range(SUBLANES):
            for k in range(TOP_K):
                fn(base + j, k)
        return carry

    lax.fori_loop(0, tb // SUBLANES, body, 0)


def _dispatch_kernel(dest_ref, m_ref, xin_ref, xbuf_ref, sem, *, tb):
    del xin_ref

    def row_copy(t, k):
        return pltpu.make_async_copy(m_ref.at[pl.ds(t, 1), :], xbuf_ref.at[pl.ds(dest_ref[t * TOP_K + k], 1), :], sem)

    _for_each_assignment(tb, lambda t, k: row_copy(t, k).start())
    _for_each_assignment(tb, lambda t, k: row_copy(t, k).wait())


def _dispatch(dest_flat, m3, xbuf, tb):
    t = m3.shape[0]
    return pl.pallas_call(
        functools.partial(_dispatch_kernel, tb=tb),
        grid=(t // tb,),
        in_specs=[pl.BlockSpec((tb * TOP_K,), lambda i: (i,), memory_space=pltpu.SMEM),
                  pl.BlockSpec((tb, D_MODEL), lambda i: (i, 0)),
                  pl.BlockSpec(memory_space=pl.ANY)],
        out_specs=pl.BlockSpec(memory_space=pl.ANY),
        out_shape=jax.ShapeDtypeStruct(xbuf.shape, xbuf.dtype),
        scratch_shapes=[pltpu.SemaphoreType.DMA(())],
        input_output_aliases={2: 0},
        compiler_params=pltpu.CompilerParams(dimension_semantics=("arbitrary",)),
        name="moe_dispatch",
    )(dest_flat, m3, xbuf)


def _split_gate_up_kernel(w_ref, wg_ref, wl_ref):
    n = 2 * LANES
    r = lax.broadcasted_iota(jnp.int32, (n, n), 0)
    c = lax.broadcasted_iota(jnp.int32, (n, n), 1)
    perm = jnp.where(r == jnp.where(c < LANES, 2 * c, 2 * (c - LANES) + 1), 1.0, 0.0).astype(BF16)
    for g in range(w_ref.shape[1] // n):
        y = _dot(w_ref[:, g * n:(g + 1) * n].astype(BF16), perm)
        wg_ref[:, g * LANES:(g + 1) * LANES] = y[:, :LANES].astype(BF16)
        wl_ref[:, g * LANES:(g + 1) * LANES] = y[:, LANES:].astype(BF16)


def _split_gate_up(w_gate_up):
    n_e, d_in, d_gu = w_gate_up.shape
    cols = 4 * LANES
    out = jax.ShapeDtypeStruct((n_e, d_in, d_gu // 2), BF16)
    return pl.pallas_call(
        _split_gate_up_kernel,
        grid=(n_e, d_gu // cols),
        in_specs=[pl.BlockSpec((None, d_in, cols), lambda e, j: (e, 0, j))],
        out_specs=[pl.BlockSpec((None, d_in, cols // 2), lambda e, j: (e, 0, j))] * 2,
        out_shape=[out, out],
        compiler_params=pltpu.CompilerParams(dimension_semantics=("arbitrary",) * 2, vmem_limit_bytes=VMEM_LIMIT),
        name="moe_split_gate_up",
    )(w_gate_up)


def _experts_kernel(be_ref, nb_ref, x_ref, wg_ref, wl_ref, wd_ref, bg_ref, bl_ref, bd_ref, y_ref):
    i = pl.program_id(0)

    @pl.when(i < nb_ref[0])
    def _():
        x = x_ref[...].astype(BF16)
        gate = jnp.minimum(_dot(x, wg_ref[...]) + bg_ref[...], SWIGLU_LIMIT)
        lin = jnp.clip(_dot(x, wl_ref[...]) + bl_ref[...], -SWIGLU_LIMIT, SWIGLU_LIMIT)
        hmid = gate * _sigmoid(SWIGLU_ALPHA * gate) * (lin + 1.0)
        y_ref[...] = _dot(hmid.astype(BF16), wd_ref[...]) + bd_ref[...]

    @pl.when(i >= nb_ref[0])
    def _():
        y_ref[...] = jnp.zeros(y_ref.shape, y_ref.dtype)


def _experts(block_e, nb_used, xbuf, wg, wl, wd, bg, bl, bd):
    n_rows = xbuf.shape[0]
    d_ff = wg.shape[2]
    xmap = lambda i, be, nb: (jnp.minimum(i, nb[0] - 1), 0)
    wmap = lambda i, be, nb: (be[i], 0, 0)
    return pl.pallas_call(
        _experts_kernel,
        grid_spec=pltpu.PrefetchScalarGridSpec(
            num_scalar_prefetch=2,
            grid=(n_rows // EXPERT_ROWS,),
            in_specs=[pl.BlockSpec((EXPERT_ROWS, D_MODEL), xmap),
                      pl.BlockSpec((None, D_MODEL, d_ff), wmap), pl.BlockSpec((None, D_MODEL, d_ff), wmap),
                      pl.BlockSpec((None, d_ff, D_MODEL), wmap),
                      pl.BlockSpec((None, 1, d_ff), wmap), pl.BlockSpec((None, 1, d_ff), wmap),
                      pl.BlockSpec((None, 1, D_MODEL), wmap)],
            out_specs=pl.BlockSpec((EXPERT_ROWS, D_MODEL), lambda i, be, nb: (i, 0))),
        out_shape=jax.ShapeDtypeStruct(xbuf.shape, F32),
        compiler_params=pltpu.CompilerParams(dimension_semantics=("arbitrary",), vmem_limit_bytes=VMEM_LIMIT),
        name="moe_experts",
    )(block_e, nb_used, xbuf, wg, wl, wd, bg, bl, bd)


def _combine_kernel(dest_ref, h_ref, tg_ref, y_ref, o_ref, buf, sem, *, tb):
    def row_copy(t, k):
        return pltpu.make_async_copy(y_ref.at[pl.ds(dest_ref[t * TOP_K + k], 1), :], buf.at[k, pl.ds(t, 1), :], sem)

    _for_each_assignment(tb, lambda t, k: row_copy(t, k).start())
    _for_each_assignment(tb, lambda t, k: row_copy(t, k).wait())
    tg = tg_ref[...]
    acc = h_ref[...]
    for k in range(TOP_K):
        acc = acc + tg[:, k:k + 1] * buf[k]
    o_ref[...] = acc


def _combine(dest_flat, h, tg, ybuf, tb):
    t = h.shape[0]
    row = lambda i: (i, 0)
    return pl.pallas_call(
        functools.partial(_combine_kernel, tb=tb),
        grid=(t // tb,),
        in_specs=[pl.BlockSpec((tb * TOP_K,), lambda i: (i,), memory_space=pltpu.SMEM),
                  pl.BlockSpec((tb, D_MODEL), row), pl.BlockSpec((tb, LANES), row),
                  pl.BlockSpec(memory_space=pl.ANY)],
        out_specs=pl.BlockSpec((tb, D_MODEL), row),
        out_shape=jax.ShapeDtypeStruct((t, D_MODEL), F32),
        scratch_shapes=[pltpu.VMEM((TOP_K, tb, D_MODEL), F32), pltpu.SemaphoreType.DMA(())],
        compiler_params=pltpu.CompilerParams(dimension_semantics=("arbitrary",), vmem_limit_bytes=VMEM_LIMIT),
        name="moe_combine",
    )(dest_flat, h, tg, ybuf)


def _routing(flat_e, n_blocks):
    onehot = (flat_e[:, None] == jnp.arange(N_EXPERTS, dtype=jnp.int32)[None, :]).astype(jnp.int32)
    csum = jnp.cumsum(onehot, axis=0)
    rank = jnp.sum(onehot * csum, axis=1) - 1
    counts = csum[-1]
    padded = (counts + EXPERT_ROWS - 1) // EXPERT_ROWS * EXPERT_ROWS
    pad_end = jnp.cumsum(padded)
    pad_start = pad_end - padded
    dest = (pad_start[flat_e] + rank).astype(jnp.int32)
    block_start = jnp.arange(n_blocks, dtype=jnp.int32) * EXPERT_ROWS
    block_e = jnp.minimum(jnp.sum((pad_end[None, :] <= block_start[:, None]).astype(jnp.int32), axis=1),
                          N_EXPERTS - 1).astype(jnp.int32)
    nb_used = (pad_end[-1:] // EXPERT_ROWS).astype(jnp.int32)
    return dest, block_e, nb_used


def kernel(x_prompt, x_sample, cache_k, cache_v, page_table, state_gdn, state_conv, norm1_g, w_in, q_norm_g,
           k_norm_g, conv_w, a_log, dt_bias, gdn_norm_g, w_out, norm2_g, w_router, b_router, w_gate_up,
           b_gate_up, w_down, b_down):
    assert w_in.shape[0] == 1, "single layer"
    b, s, d = x_prompt.shape
    db, dl, _ = x_sample.shape
    tp, ts = b * s, db * dl

    w_in_p = jnp.pad(w_in[0], ((0, 0), (0, IN_COLS_PAD - IN_COLS))).astype(BF16)
    qg = jnp.tile(q_norm_g[0], N_HEADS)[None, :]
    kg = jnp.tile(k_norm_g[0], N_HEADS)[None, :]
    neg_a = jnp.pad(-jnp.exp(a_log[0]), (0, LANES - N_HEADS))[None, :]
    dtb = jnp.pad(dt_bias[0], (0, LANES - N_HEADS))[None, :]
    ng = jnp.tile(gdn_norm_g[0], 2)[None, :]
    wo = w_out[0].astype(BF16)
    wr = jnp.pad(w_router[0], ((0, 0), (0, LANES - N_EXPERTS)))
    br = jnp.pad(b_router[0], (0, LANES - N_EXPERTS), constant_values=NEG)[None, :]
    wg, wl = _split_gate_up(w_gate_up[0])
    bg = b_gate_up[0][:, None, 0::2]
    bl = b_gate_up[0][:, None, 1::2]
    wd = w_down[0].astype(BF16)
    bd = b_down[0][:, None, :]
    slopes = (2.0 ** (-8.0 * jnp.arange(1, N_HEADS + 1, dtype=F32) / N_HEADS)).astype(F32)

    xp2 = x_prompt.reshape(tp, d)
    xs2 = x_sample.reshape(ts, d)
    g1 = norm1_g[0][None, :]
    g2 = norm2_g[0][None, :]

    qp, kp, vp, gqp, zp, gbp = _inproj(xp2, g1, w_in_p, qg, kg, neg_a, dtb, tm=512)
    att_p = _moba_prompt(slopes, qp, kp, vp, b, s)
    gdn_p, gstate_p = _gdn(gqp.reshape(b, s, GDN_QKV), zp.reshape(b, s, WIDTH), gbp.reshape(b, s, LANES),
                           conv_w[0], jnp.zeros((b, CONV_W - 1, GDN_QKV), F32),
                           jnp.zeros((b, N_HEADS, HEAD_DIM, HEAD_DIM), F32), ng, b, s, s)

    qs, ks, vs, gqs, zs, gbs = _inproj(xs2, g1, w_in_p, qg, kg, neg_a, dtb, tm=ts)
    page_major = lambda c: jnp.transpose(c[0], (0, 2, 3, 1)).reshape(c.shape[1], WIDTH, PAGE)
    att_s = _moba_sample(page_table, qs, ks, vs, page_major(cache_k), page_major(cache_v), db, dl)
    padl = lambda a: jnp.pad(a.reshape(db, dl, -1), ((0, 0), (0, GDN_CHUNK - dl), (0, 0)))
    gdn_s, gstate_s = _gdn(padl(gqs), padl(zs), padl(gbs), conv_w[0], state_conv[0], state_gdn[0], ng,
                           db, GDN_CHUNK, dl)
    gdn_s = gdn_s[:, :dl].reshape(ts, WIDTH)

    hp, mp, tip, tgp = _outproj(xp2, att_p, gdn_p.reshape(tp, WIDTH), wo, g2, wr, br, tm=512)
    hs, msm, tis, tgs = _outproj(xs2, att_s, gdn_s, wo, g2, wr, br, tm=ts)

    n_assign = (tp + ts) * TOP_K
    n_blocks = (n_assign + N_EXPERTS * (EXPERT_ROWS - 1)) // EXPERT_ROWS + 1
    flat_e = jnp.concatenate([tip[:, :TOP_K].reshape(-1), tis[:, :TOP_K].reshape(-1)])
    dest, block_e, nb_used = _routing(flat_e, n_blocks)
    dest_p, dest_s = dest[:tp * TOP_K], dest[tp * TOP_K:]
    xbuf = jnp.zeros((n_blocks * EXPERT_ROWS, D_MODEL), F32)
    xbuf = _dispatch(dest_p, mp, xbuf, tb=256)
    xbuf = _dispatch(dest_s, msm, xbuf, tb=ts)
    ybuf = _experts(block_e, nb_used, xbuf, wg, wl, wd, bg, bl, bd)
    y_prompt = _combine(dest_p, hp, tgp, ybuf, tb=256).reshape(b, s, d)
    y_sample = _combine(dest_s, hs, tgs, ybuf, tb=ts).reshape(db, dl, d)

    k_prompt = kp.reshape(1, b, s // PAGE, PAGE, N_HEADS, HEAD_DIM)
    v_prompt = vp.reshape(1, b, s // PAGE, PAGE, N_HEADS, HEAD_DIM)
    k_sample = ks.reshape(1, db, dl, N_HEADS, HEAD_DIM)
    v_sample = vs.reshape(1, db, dl, N_HEADS, HEAD_DIM)
    conv_prompt = gqp.reshape(b, s, GDN_QKV)[:, s - (CONV_W - 1):][None]
    conv_sample = jnp.concatenate([state_conv[0], gqs.reshape(db, dl, GDN_QKV)], axis=1)[:, -(CONV_W - 1):][None]
    return (y_prompt, y_sample, k_prompt, v_prompt, k_sample, v_sample,
            gstate_p[None], conv_prompt, gstate_s[None], conv_sample)
```
